```python
import jax, jax.numpy as jnp
from jax import lax
import numpy as np

D_MODEL = 1024
BATCH = 2
SEQ = 16384
DEPTH = 4
DEC_BATCH = 16
DEC_SEQ = 16
PAST_LEN = 1024

CHUNK = 64
N_MIXERS = 2
N_RET_LAYERS = (DEPTH + 1) // 2
N_POOL_LAYERS = DEPTH // 2
RET_HEADS = 4
RET_DK = D_MODEL // RET_HEADS
RET_DV = 2 * RET_DK
RET_QK = RET_HEADS * RET_DK
RET_VDIM = RET_HEADS * RET_DV
RET_IN = 2 * RET_QK + 2 * RET_VDIM
ROPE_BASE = 10000.0
POOL_WINDOWS = (2, 4, 8, 16)
POOL_GROUPS = len(POOL_WINDOWS)
POOL_GW = D_MODEL // POOL_GROUPS
POOL_BUF = max(POOL_WINDOWS) - 1
D_FF = -(-8 * D_MODEL // (3 * 256)) * 256
EPS = 1e-6

kernel_name = "retnet_pool_interleaved_stream_step"


def _rmsnorm(x, g):
    xf = x.astype(jnp.float32)
    y = xf * lax.rsqrt(jnp.mean(xf * xf, axis=-1, keepdims=True) + EPS)
    return (y * g.astype(jnp.float32)).astype(x.dtype)


def _rotary(x, pos):
    half = x.shape[-1] // 2
    inv = ROPE_BASE ** (-jnp.arange(half, dtype=jnp.float32) / half)
    ang = pos.astype(jnp.float32)[:, None] * inv[None, :]
    cos, sin = jnp.cos(ang), jnp.sin(ang)
    x1, x2 = x[..., :half], x[..., half:]
    return jnp.concatenate([x1 * cos - x2 * sin, x1 * sin + x2 * cos], axis=-1)


def _log_gamma():
    return jnp.log1p(-jnp.exp2(-5.0 - jnp.arange(RET_HEADS, dtype=jnp.float32)))[:, None]


def _retention_block(q, k, v, S, lg):
    C = q.shape[2]
    idx = jnp.arange(C, dtype=jnp.float32)
    diff = idx[:, None] - idx[None, :]
    dmask = jnp.where(diff >= 0, jnp.exp(lg[:, :, None] * jnp.maximum(diff, 0.0)), 0.0)
    scores = jnp.einsum('bhnd,bhmd->bhnm', q, k) * dmask
    inner = jnp.einsum('bhnm,bhme->bhne', scores, v)
    q_dec = q * jnp.exp(lg * (idx + 1.0))[None, :, :, None]
    cross = jnp.einsum('bhnd,bhde->bhne', q_dec, S)
    k_dec = k * jnp.exp(lg * (C - 1.0 - idx))[None, :, :, None]
    S_new = jnp.exp(lg * C)[None, :, :, None] * S + jnp.einsum('bhmd,bhme->bhde', k_dec, v)
    return inner + cross, S_new


def _retention(h, S0, pos, w_in, w_out):
    B, L, _ = h.shape
    proj = h @ w_in
    q, k, v, g = jnp.split(proj, [RET_QK, 2 * RET_QK, 2 * RET_QK + RET_VDIM], axis=-1)

    def heads(t, dh):
        return t.reshape(B, L, RET_HEADS, dh).transpose(0, 2, 1, 3).astype(jnp.float32)

    q = _rotary(heads(q, RET_DK), pos)
    k = _rotary(heads(k, RET_DK), pos) * (RET_DK ** -0.5)
    v = heads(v, RET_DV)
    nc = max(L // CHUNK, 1)
    c = L // nc

    def to_blocks(t):
        return t.reshape(B, RET_HEADS, nc, c, t.shape[-1]).transpose(2, 0, 1, 3, 4)

    lg = _log_gamma()

    def step(S, blk):
        qb, kb, vb = blk
        o, S = _retention_block(qb, kb, vb, S, lg)
        return S, o

    S_fin, o = lax.scan(step, S0.astype(jnp.float32), (to_blocks(q), to_blocks(k), to_blocks(v)))
    o = o.transpose(1, 2, 0, 3, 4).reshape(B, RET_HEADS, L, RET_DV)
    o = o * lax.rsqrt(jnp.mean(o * o, axis=-1, keepdims=True) + EPS)
    o = o.transpose(0, 2, 1, 3).reshape(B, L, RET_VDIM)
    y = (jax.nn.silu(g.astype(jnp.float32)) * o).astype(h.dtype) @ w_out
    return y, S_fin


def _pool(h, buf, pos, w_pool, scale):
    B, L, D = h.shape
    xp = jnp.concatenate([buf.astype(h.dtype), h], axis=1)
    cs = jnp.cumsum(xp.astype(jnp.float32), axis=1)
    cs = jnp.concatenate([jnp.zeros((B, 1, D), jnp.float32), cs], axis=1)
    hf = h.astype(jnp.float32)
    start = POOL_BUF + 1
    diffs = []
    for gi, w in enumerate(POOL_WINDOWS):
        sl = slice(gi * POOL_GW, (gi + 1) * POOL_GW)
        wsum = cs[:, start:start + L, sl] - cs[:, start - w:start - w + L, sl]
        cnt = jnp.minimum(pos + 1, w).astype(jnp.float32)[None, :, None]
        diffs.append(wsum / cnt - hf[:, :, sl])
    d = jnp.stack(diffs, axis=2).astype(h.dtype)
    y = jnp.einsum('blgc,gce->blge', d, w_pool).reshape(B, L, D) * scale
    return y, xp[:, -POOL_BUF:]


def _swiglu(h, w_in, w_out):
    a, b = jnp.split(h @ w_in, 2, axis=-1)
    return (jax.nn.silu(a) * b) @ w_out


def _trunk(x, pos, ret_states, pool_bufs, norm_mix, w_ret_in, w_ret_out, w_pool, pool_scale,
           norm_ffn, w_ffn_in, w_ffn_out, norm_final):
    new_ret, new_pool = [], []
    for i in range(DEPTH):
        j = i // N_MIXERS
        h = _rmsnorm(x, norm_mix[i])
        if i % N_MIXERS == 0:
            y, s = _retention(h, ret_states[j], pos, w_ret_in[j], w_ret_out[j])
            new_ret.append(s.astype(ret_states.dtype))
        else:
            y, b = _pool(h, pool_bufs[j], pos, w_pool[j], pool_scale[j])
            new_pool.append(b.astype(pool_bufs.dtype))
        x = x + y
        x = x + _swiglu(_rmsnorm(x, norm_ffn[i]), w_ffn_in[i], w_ffn_out[i])
    return _rmsnorm(x, norm_final), jnp.stack(new_ret), jnp.stack(new_pool)


def setup_inputs(seed: int = 0) -> dict:
    key = jax.random.key(seed)
    ks = jax.random.split(key, 14)
    f32 = jnp.float32
    nrm = lambda k, s, sc: jax.random.normal(k, s, f32) * sc
    return {
        "x_prompt": nrm(ks[0], (BATCH, SEQ, D_MODEL), 1.0),
        "x_sample": nrm(ks[1], (DEC_BATCH, DEC_SEQ, D_MODEL), 1.0),
        "state_ret": nrm(ks[2], (N_RET_LAYERS, DEC_BATCH, RET_HEADS, RET_DK, RET_DV), 0.5),
        "state_pool": nrm(ks[3], (N_POOL_LAYERS, DEC_BATCH, POOL_BUF, D_MODEL), 1.0),
        "norm_mix": 1.0 + nrm(ks[4], (DEPTH, D_MODEL), 0.1),
        "w_ret_in": nrm(ks[5], (N_RET_LAYERS, D_MODEL, RET_IN), D_MODEL ** -0.5),
        "w_ret_out": nrm(ks[6], (N_RET_LAYERS, RET_VDIM, D_MODEL), RET_VDIM ** -0.5),
        "w_pool": nrm(ks[7], (N_POOL_LAYERS, POOL_GROUPS, POOL_GW, POOL_GW), POOL_GW ** -0.5),
        "pool_scale": 1.0 + nrm(ks[8], (N_POOL_LAYERS, D_MODEL), 0.1),
        "norm_ffn": 1.0 + nrm(ks[9], (DEPTH, D_MODEL), 0.1),
        "w_ffn_in": nrm(ks[10], (DEPTH, D_MODEL, 2 * D_FF), D_MODEL ** -0.5),
        "w_ffn_out": nrm(ks[11], (DEPTH, D_FF, D_MODEL), D_FF ** -0.5),
        "norm_final": 1.0 + nrm(ks[12], (D_MODEL,), 0.1),
    }


def reference(x_prompt, x_sample, state_ret, state_pool, norm_mix, w_ret_in, w_ret_out, w_pool,
              pool_scale, norm_ffn, w_ffn_in, w_ffn_out, norm_final):
    Bp, Lp, _ = x_prompt.shape
    Bs, Ls, _ = x_sample.shape
    pos_p = jnp.arange(Lp, dtype=jnp.int32)
    pos_s = PAST_LEN + jnp.arange(Ls, dtype=jnp.int32)
    ret0 = jnp.zeros((N_RET_LAYERS, Bp, RET_HEADS, RET_DK, RET_DV), state_ret.dtype)
    pool0 = jnp.zeros((N_POOL_LAYERS, Bp, POOL_BUF, D_MODEL), state_pool.dtype)
    y_prompt, ret_prompt_new, pool_prompt_new = _trunk(
        x_prompt, pos_p, ret0, pool0, norm_mix, w_ret_in, w_ret_out, w_pool, pool_scale,
        norm_ffn, w_ffn_in, w_ffn_out, norm_final)
    y_sample, ret_sample_new, pool_sample_new = _trunk(
        x_sample, pos_s, state_ret, state_pool, norm_mix, w_ret_in, w_ret_out, w_pool, pool_scale,
        norm_ffn, w_ffn_in, w_ffn_out, norm_final)
    return (y_prompt, y_sample, ret_prompt_new, ret_sample_new, pool_prompt_new, pool_sample_new)
```

```python
import functools
import math

import jax
import jax.numpy as jnp
from jax import lax
from jax.experimental import pallas as pl
from jax.experimental.pallas import tpu as pltpu

F32 = jnp.float32
BF16 = jnp.bfloat16

EPS = 1e-6
ROPE_BASE = 10000.0
PAST_LEN = 1024
RET_HEADS = 4
POOL_WINDOWS = (2, 4, 8, 16)
POOL_HIST = 16
PROMPT_TILE = 256
FFN_TILE = 512
VMEM_LIMIT_BYTES = 56 * 1024 * 1024


def _log_gamma(head):
    return math.log1p(-(2.0 ** (-5.0 - head)))


def _rms(x, g):
    return (x * lax.rsqrt(jnp.mean(x * x, axis=-1, keepdims=True) + EPS)) * g


def _dot(a, b):
    return jnp.dot(a, b, preferred_element_type=F32)


def _dot_nt(a, b):
    return lax.dot_general(a, b, (((1,), (1,)), ((), ())), preferred_element_type=F32)


def _dot_tn(a, b):
    return lax.dot_general(a, b, (((0,), (0,)), ((), ())), preferred_element_type=F32)


def _rotary(t, cos, sin):
    half = t.shape[-1] // 2
    t1, t2 = t[:, :half], t[:, half:]
    return jnp.concatenate([t1 * cos - t2 * sin, t1 * sin + t2 * cos], axis=-1)


def _silu(t):
    return t * jax.nn.sigmoid(t)


def _decay_mask(lg, n):
    r = lax.broadcasted_iota(jnp.int32, (n, n), 0)
    c = lax.broadcasted_iota(jnp.int32, (n, n), 1)
    diff = (r - c).astype(F32)
    return jnp.where(diff >= 0, jnp.exp(lg * jnp.maximum(diff, 0.0)), 0.0)


def _ffn(x, nf, win_ref, wout_ref):
    d_ff = wout_ref.shape[0]
    h = _rms(x, nf).astype(BF16)
    u = _dot(h, win_ref[...])
    act = (_silu(u[:, :d_ff]) * u[:, d_ff:]).astype(BF16)
    return x + _dot(act, wout_ref[...])


def _const_spec(shape):
    nd = len(shape)
    return pl.BlockSpec(shape, lambda *_: (0,) * nd, pipeline_mode=pl.Buffered(1))


def _params(n_axes):
    return pltpu.CompilerParams(dimension_semantics=("arbitrary",) * n_axes,
                                vmem_limit_bytes=VMEM_LIMIT_BYTES)


def _ffn_kernel(x_ref, nf_ref, win_ref, wout_ref, o_ref):
    o_ref[...] = _ffn(x_ref[...], nf_ref[...], win_ref, wout_ref)


def _ffn_call(x2d, nf, win, wout, tile):
    n, d = x2d.shape
    return pl.pallas_call(
        _ffn_kernel,
        grid=(n // tile,),
        in_specs=[pl.BlockSpec((tile, d), lambda i: (i, 0)),
                  _const_spec(nf.shape), _const_spec(win.shape), _const_spec(wout.shape)],
        out_specs=pl.BlockSpec((tile, d), lambda i: (i, 0)),
        out_shape=jax.ShapeDtypeStruct((n, d), F32),
        compiler_params=_params(1),
        name="ffn",
    )(x2d, nf, win, wout)


def _ret_head_proj(h, win_ref, hd, dk, dv):
    qk = RET_HEADS * dk
    vd = RET_HEADS * dv
    q = _dot(h, win_ref[:, hd * dk:(hd + 1) * dk])
    k = _dot(h, win_ref[:, qk + hd * dk:qk + (hd + 1) * dk]) * (dk ** -0.5)
    v = _dot(h, win_ref[:, 2 * qk + hd * dv:2 * qk + (hd + 1) * dv])
    gate = _dot(h, win_ref[:, 2 * qk + vd + hd * dv:2 * qk + vd + (hd + 1) * dv])
    return q, k, v, gate


def _ret_prompt_kernel(x_ref, cos_ref, sin_ref, s0_ref, g_ref, win_ref, wout_ref,
                       xo_ref, s_ref, dmask_ref, og_ref):
    c = x_ref.shape[1]
    dk, dv = s_ref.shape[2], s_ref.shape[3]
    j = pl.program_id(1)

    @pl.when((pl.program_id(0) == 0) & (j == 0))
    def _():
        for hd in range(RET_HEADS):
            dmask_ref[hd] = _decay_mask(_log_gamma(hd), c)

    @pl.when(j == 0)
    def _():
        s_ref[...] = s0_ref[...]

    x = x_ref[0]
    h = _rms(x, g_ref[...]).astype(BF16)
    cos, sin = cos_ref[...], sin_ref[...]
    idx = lax.broadcasted_iota(jnp.int32, (c, 1), 0).astype(F32)
    for hd in range(RET_HEADS):
        lg = _log_gamma(hd)
        q, k, v, gate = _ret_head_proj(h, win_ref, hd, dk, dv)
        q = _rotary(q, cos, sin)
        k = _rotary(k, cos, sin)
        v = v.astype(BF16)
        q_dec = (q * jnp.exp(lg * (idx + 1.0))).astype(BF16)
        k_dec = k * jnp.exp(lg * (c - 1.0 - idx))
        scores = _dot_nt(q.astype(BF16), k.astype(BF16)) * dmask_ref[hd]
        s_old = s_ref[0, hd]
        lhs = jnp.concatenate([scores.astype(BF16), q_dec], axis=1)
        rhs = jnp.concatenate([v, s_old.astype(BF16)], axis=0)
        o = _dot(lhs, rhs)
        s_ref[0, hd] = math.exp(lg * c) * s_old + _dot(k_dec.T.astype(BF16), v)
        o = o * lax.rsqrt(jnp.mean(o * o, axis=-1, keepdims=True) + EPS)
        og_ref[:, hd * dv:(hd + 1) * dv] = (_silu(gate) * o).astype(BF16)
    xo_ref[0] = x + _dot(og_ref[...], wout_ref[...])


def _ret_prompt_call(x, cos, sin, s0, g, win, wout):
    b, l, d = x.shape
    _, nh, dk, dv = s0.shape
    c = PROMPT_TILE
    half = cos.shape[1]
    return pl.pallas_call(
        _ret_prompt_kernel,
        grid=(b, l // c),
        in_specs=[pl.BlockSpec((1, c, d), lambda i, j: (i, j, 0)),
                  pl.BlockSpec((c, half), lambda i, j: (j, 0)),
                  pl.BlockSpec((c, half), lambda i, j: (j, 0)),
                  pl.BlockSpec((1, nh, dk, dv), lambda i, j: (i, 0, 0, 0)),
                  _const_spec(g.shape), _const_spec(win.shape), _const_spec(wout.shape)],
        out_specs=[pl.BlockSpec((1, c, d), lambda i, j: (i, j, 0)),
                   pl.BlockSpec((1, nh, dk, dv), lambda i, j: (i, 0, 0, 0))],
        out_shape=[jax.ShapeDtypeStruct((b, l, d), F32),
                   jax.ShapeDtypeStruct((b, nh, dk, dv), F32)],
        scratch_shapes=[pltpu.VMEM((nh, c, c), F32),
                        pltpu.VMEM((c, nh * dv), BF16)],
        compiler_params=_params(2),
        name="ret_prompt",
    )(x, cos, sin, s0, g, win, wout)


def _ret_sample_kernel(x_ref, cos_ref, sin_ref, s0_ref, g_ref, win_ref, wout_ref,
                       xo_ref, s_ref, q_ref, k_ref, v_ref, gate_ref, og_ref, *, seg):
    dk, dv = s_ref.shape[2], s_ref.shape[3]
    s = pl.program_id(0)

    @pl.when(s == 0)
    def _():
        h = _rms(x_ref[...], g_ref[...]).astype(BF16)
        cos, sin = cos_ref[...], sin_ref[...]
        for hd in range(RET_HEADS):
            q, k, v, gate = _ret_head_proj(h, win_ref, hd, dk, dv)
            q_ref[:, hd * dk:(hd + 1) * dk] = _rotary(q, cos, sin)
            k_ref[:, hd * dk:(hd + 1) * dk] = _rotary(k, cos, sin)
            v_ref[:, hd * dv:(hd + 1) * dv] = v
            gate_ref[:, hd * dv:(hd + 1) * dv] = gate

    rows = pl.ds(pl.multiple_of(s * seg, seg), seg)
    idx = lax.broadcasted_iota(jnp.int32, (seg, 1), 0).astype(F32)
    for hd in range(RET_HEADS):
        lg = _log_gamma(hd)
        q = q_ref[rows, hd * dk:(hd + 1) * dk]
        k = k_ref[rows, hd * dk:(hd + 1) * dk]
        v = v_ref[rows, hd * dv:(hd + 1) * dv].astype(BF16)
        gate = gate_ref[rows, hd * dv:(hd + 1) * dv]
        q_dec = (q * jnp.exp(lg * (idx + 1.0))).astype(BF16)
        k_dec = (k * jnp.exp(lg * (seg - 1.0 - idx))).astype(BF16)
        scores = _dot_nt(q.astype(BF16), k.astype(BF16)) * _decay_mask(lg, seg)
        s_old = s0_ref[0, hd]
        o = _dot(scores.astype(BF16), v) + _dot(q_dec, s_old.astype(BF16))
        s_ref[0, hd] = math.exp(lg * seg) * s_old + _dot_tn(k_dec, v)
        o = o * lax.rsqrt(jnp.mean(o * o, axis=-1, keepdims=True) + EPS)
        og_ref[rows, hd * dv:(hd + 1) * dv] = _silu(gate) * o

    @pl.when(s == pl.num_programs(0) - 1)
    def _():
        xo_ref[...] = x_ref[...] + _dot(og_ref[...].astype(BF16), wout_ref[...])


def _ret_sample_call(x, cos, sin, s0, g, win, wout):
    ns, seg, d = x.shape
    _, nh, dk, dv = s0.shape
    n = ns * seg
    out, s_new = pl.pallas_call(
        functools.partial(_ret_sample_kernel, seg=seg),
        grid=(ns,),
        in_specs=[_const_spec((n, d)), _const_spec(cos.shape), _const_spec(sin.shape),
                  pl.BlockSpec((1, nh, dk, dv), lambda i: (i, 0, 0, 0)),
                  _const_spec(g.shape), _const_spec(win.shape), _const_spec(wout.shape)],
        out_specs=[pl.BlockSpec((n, d), lambda i: (0, 0)),
                   pl.BlockSpec((1, nh, dk, dv), lambda i: (i, 0, 0, 0))],
        out_shape=[jax.ShapeDtypeStruct((n, d), F32),
                   jax.ShapeDtypeStruct((ns, nh, dk, dv), F32)],
        scratch_shapes=[pltpu.VMEM((n, nh * dk), F32), pltpu.VMEM((n, nh * dk), F32),
                        pltpu.VMEM((n, nh * dv), F32), pltpu.VMEM((n, nh * dv), F32),
                        pltpu.VMEM((n, nh * dv), F32)],
        compiler_params=_params(1),
        name="ret_sample",
    )(x.reshape(n, d), cos, sin, s0, g, win, wout)
    return out.reshape(ns, seg, d), s_new


def _pool_ffn_kernel(x_ref, buf_ref, nm_ref, wp_ref, ps_ref, nf_ref, win_ref, wout_ref,
                     nfin_ref, xo_ref, bo_ref, xp_ref, *, pos0, final):
    nseg, lt, d = x_ref.shape
    gw = d // len(POOL_WINDOWS)
    j = pl.program_id(1)

    @pl.when(j == 0)
    def _():
        xp_ref[:, 0:POOL_HIST, :] = buf_ref[...]

    x = x_ref[...]
    h = _rms(x, nm_ref[...])
    xp_ref[:, POOL_HIST:, :] = h
    pos = pos0 + j * lt + lax.broadcasted_iota(jnp.int32, (1, lt, 1), 1)
    ys = []
    for gi, w in enumerate(POOL_WINDOWS):
        cols = slice(gi * gw, (gi + 1) * gw)
        hg = h[:, :, cols]
        wsum = hg
        for back in range(1, w):
            wsum = wsum + xp_ref[:, POOL_HIST - back:POOL_HIST - back + lt, cols]
        cnt = jnp.minimum(pos + 1, w).astype(F32)
        dg = (wsum / cnt - hg).astype(BF16).reshape(nseg * lt, gw)
        ys.append(_dot(dg, wp_ref[gi]))
    y = jnp.concatenate(ys, axis=-1) * ps_ref[...]
    x1 = x.reshape(nseg * lt, d) + y
    x2 = _ffn(x1, nf_ref[...], win_ref, wout_ref)
    if final:
        x2 = _rms(x2, nfin_ref[...])
    xo_ref[...] = x2.reshape(nseg, lt, d)
    tail = xp_ref[:, lt:lt + POOL_HIST, :]
    bo_ref[...] = tail
    xp_ref[:, 0:POOL_HIST, :] = tail


def _pool_ffn_call(x, buf16, nm, wp, ps, nf, win, wout, nfin, *, seg_block, tile, pos0, final):
    ns, l, d = x.shape
    return pl.pallas_call(
        functools.partial(_pool_ffn_kernel, pos0=pos0, final=final),
        grid=(ns // seg_block, l // tile),
        in_specs=[pl.BlockSpec((seg_block, tile, d), lambda i, j: (i, j, 0)),
                  pl.BlockSpec((seg_block, POOL_HIST, d), lambda i, j: (i, 0, 0)),
                  _const_spec(nm.shape), _const_spec(wp.shape), _const_spec(ps.shape),
                  _const_spec(nf.shape), _const_spec(win.shape), _const_spec(wout.shape),
                  _const_spec(nfin.shape)],
        out_specs=[pl.BlockSpec((seg_block, tile, d), lambda i, j: (i, j, 0)),
                   pl.BlockSpec((seg_block, POOL_HIST, d), lambda i, j: (i, 0, 0))],
        out_shape=[jax.ShapeDtypeStruct((ns, l, d), F32),
                   jax.ShapeDtypeStruct((ns, POOL_HIST, d), F32)],
        scratch_shapes=[pltpu.VMEM((seg_block, POOL_HIST + tile, d), F32)],
        compiler_params=_params(2),
        name="pool_ffn",
    )(x, buf16, nm, wp, ps, nf, win, wout, nfin)


def _rope_tables(pos, half):
    inv = ROPE_BASE ** (-jnp.arange(half, dtype=F32) / half)
    ang = pos.astype(F32)[:, None] * inv[None, :]
    return jnp.cos(ang), jnp.sin(ang)


def kernel(x_prompt, x_sample, state_ret, state_pool, norm_mix, w_ret_in, w_ret_out, w_pool,
           pool_scale, norm_ffn, w_ffn_in, w_ffn_out, norm_final):
    bp, lp, d = x_prompt.shape
    bs, ls, _ = x_sample.shape
    depth = norm_mix.shape[0]
    _, _, nh, dk, dv = state_ret.shape
    n_pool = state_pool.shape[0]
    assert nh == RET_HEADS and state_pool.shape[2] == POOL_HIST - 1
    assert lp % PROMPT_TILE == 0 and (bp * lp) % FFN_TILE == 0 and ls >= POOL_HIST

    w_ret_in_b = w_ret_in.astype(BF16)
    w_ret_out_b = w_ret_out.astype(BF16)
    w_pool_b = w_pool.astype(BF16)
    w_ffn_in_b = w_ffn_in.astype(BF16)
    w_ffn_out_b = w_ffn_out.astype(BF16)

    cos_p, sin_p = _rope_tables(jnp.arange(lp, dtype=jnp.int32), dk // 2)
    cos_s, sin_s = _rope_tables(PAST_LEN + jnp.arange(ls, dtype=jnp.int32), dk // 2)
    cos_s, sin_s = jnp.tile(cos_s, (bs, 1)), jnp.tile(sin_s, (bs, 1))

    ret0_p = jnp.zeros((bp, nh, dk, dv), state_ret.dtype)
    pool0_p = jnp.zeros((bp, POOL_HIST, d), state_pool.dtype)
    pool0_s = jnp.pad(state_pool, ((0, 0), (0, 0), (1, 0), (0, 0)))

    xp, xs = x_prompt, x_sample
    ret_p, ret_s, pool_p, pool_s = [], [], [], []
    for i in range(depth):
        jm = i // 2
        nm = norm_mix[i][None, :]
        nf = norm_ffn[i][None, :]
        if i % 2 == 0:
            xp, sp = _ret_prompt_call(xp, cos_p, sin_p, ret0_p, nm, w_ret_in_b[jm], w_ret_out_b[jm])
            xs, ss = _ret_sample_call(xs, cos_s, sin_s, state_ret[jm], nm, w_ret_in_b[jm],
                                      w_ret_out_b[jm])
            ret_p.append(sp)
            ret_s.append(ss)
            xp = _ffn_call(xp.reshape(bp * lp, d), nf, w_ffn_in_b[i], w_ffn_out_b[i],
                           FFN_TILE).reshape(bp, lp, d)
            xs = _ffn_call(xs.reshape(bs * ls, d), nf, w_ffn_in_b[i], w_ffn_out_b[i],
                           bs * ls).reshape(bs, ls, d)
        else:
            final = i == depth - 1
            args = (nm, w_pool_b[jm], pool_scale[jm][None, :], nf, w_ffn_in_b[i], w_ffn_out_b[i],
                    norm_final[None, :])
            xp, bpn = _pool_ffn_call(xp, pool0_p, *args, seg_block=1, tile=PROMPT_TILE, pos0=0,
                                     final=final)
            xs, bsn = _pool_ffn_call(xs, pool0_s[jm], *args, seg_block=bs, tile=ls, pos0=PAST_LEN,
                                     final=final)
            pool_p.append(bpn[:, 1:, :])
            pool_s.append(bsn[:, 1:, :])
    assert depth % 2 == 0
    return (xp, xs, jnp.stack(ret_p), jnp.stack(ret_s), jnp.stack(pool_p), jnp.stack(pool_s))
```

```python
import functools
import math

import numpy as np

import jax
import jax.numpy as jnp
from jax import lax
from jax.experimental import pallas as pl
from jax.experimental.pallas import tpu as pltpu

F32 = jnp.float32
BF16 = jnp.bfloat16

EPS = 1e-6
ROPE_BASE = 10000.0
PAST_LEN = 1024
RET_HEADS = 4
POOL_WINDOWS = (2, 4, 8, 16)
POOL_HIST = 16
PROMPT_TILE = 512
RET_CHUNK = 256
VMEM_LIMIT_BYTES = 56 * 1024 * 1024


def _log_gamma(head):
    return math.log1p(-(2.0 ** (-5.0 - head)))


def _rms(x, g):
    return (x * lax.rsqrt(jnp.mean(x * x, axis=-1, keepdims=True) + EPS)) * g


def _dot(a, b):
    return jnp.dot(a, b, preferred_element_type=F32)


def _dot_nt(a, b):
    return lax.dot_general(a, b, (((1,), (1,)), ((), ())), preferred_element_type=F32)


def _dot_tn(a, b):
    return lax.dot_general(a, b, (((0,), (0,)), ((), ())), preferred_element_type=F32)


def _rotary(t, cos, sin):
    half = t.shape[-1] // 2
    t1, t2 = t[:, :half], t[:, half:]
    return jnp.concatenate([t1 * cos - t2 * sin, t1 * sin + t2 * cos], axis=-1)


def _silu(t):
    return t * jax.nn.sigmoid(t)


def _decay_mask(lg, n):
    r = lax.broadcasted_iota(jnp.int32, (n, n), 0)
    c = lax.broadcasted_iota(jnp.int32, (n, n), 1)
    diff = (r - c).astype(F32)
    return jnp.where(diff >= 0, jnp.exp(lg * jnp.maximum(diff, 0.0)), 0.0)


def _ffn(x, nf, win_ref, wout_ref):
    d_ff = wout_ref.shape[0]
    h = _rms(x, nf).astype(BF16)
    u = _dot(h, win_ref[...])
    act = (_silu(u[:, :d_ff]) * u[:, d_ff:]).astype(BF16)
    return x + _dot(act, wout_ref[...])


def _const_spec(shape):
    nd = len(shape)
    return pl.BlockSpec(shape, lambda *_: (0,) * nd, pipeline_mode=pl.Buffered(1))


def _layer_spec(stacked_shape, layer):
    nd = len(stacked_shape)
    return pl.BlockSpec((None,) + tuple(stacked_shape[1:]),
                        lambda *_: (layer,) + (0,) * (nd - 1), pipeline_mode=pl.Buffered(1))


def _params(n_axes):
    return pltpu.CompilerParams(dimension_semantics=("arbitrary",) * n_axes,
                                vmem_limit_bytes=VMEM_LIMIT_BYTES)


def _ffn_kernel(x_ref, nf_ref, win_ref, wout_ref, o_ref):
    o_ref[...] = _ffn(x_ref[...], nf_ref[...], win_ref, wout_ref)


def _ffn_call(x2d, nf, win, wout, layer, tile):
    n, d = x2d.shape
    return pl.pallas_call(
        _ffn_kernel,
        grid=(n // tile,),
        in_specs=[pl.BlockSpec((tile, d), lambda i: (i, 0)),
                  _layer_spec(nf.shape, layer), _layer_spec(win.shape, layer),
                  _layer_spec(wout.shape, layer)],
        out_specs=pl.BlockSpec((tile, d), lambda i: (i, 0)),
        out_shape=jax.ShapeDtypeStruct((n, d), F32),
        compiler_params=_params(1),
        name="ffn",
    )(x2d, nf, win, wout)


def _ret_head_proj(h, win_ref, hd, dk, dv):
    qk = RET_HEADS * dk
    vd = RET_HEADS * dv
    q = _dot(h, win_ref[:, hd * dk:(hd + 1) * dk])
    k = _dot(h, win_ref[:, qk + hd * dk:qk + (hd + 1) * dk]) * (dk ** -0.5)
    v = _dot(h, win_ref[:, 2 * qk + hd * dv:2 * qk + (hd + 1) * dv])
    gate = _dot(h, win_ref[:, 2 * qk + vd + hd * dv:2 * qk + vd + (hd + 1) * dv])
    return q, k, v, gate


def _ret_prompt_kernel(x_ref, cosr_ref, sinr_ref, cosb_ref, sinb_ref, g_ref, win_ref, wout_ref,
                       xo_ref, s_ref, dmask_ref, qdec_ref, kdec_ref, og_ref):
    t = x_ref.shape[1]
    c = dmask_ref.shape[1]
    dk, dv = s_ref.shape[2], s_ref.shape[3]
    j = pl.program_id(1)

    @pl.when((pl.program_id(0) == 0) & (j == 0))
    def _():
        idx = lax.broadcasted_iota(jnp.int32, (c, dk), 0).astype(F32)
        for hd in range(RET_HEADS):
            lg = _log_gamma(hd)
            dmask_ref[hd] = _decay_mask(lg, c)
            qdec_ref[hd] = jnp.exp(lg * (idx + 1.0))
            kdec_ref[hd] = jnp.exp(lg * (c - 1.0 - idx))

    @pl.when(j == 0)
    def _():
        s_ref[...] = jnp.zeros(s_ref.shape, F32)

    x = x_ref[0]
    h = _rms(x, g_ref[...]).astype(BF16)
    cosb, sinb = cosb_ref[...], sinb_ref[...]
    cosr, sinr = cosr_ref[...], sinr_ref[...]
    cos = cosb * cosr - sinb * sinr
    sin = sinb * cosr + cosb * sinr
    for hd in range(RET_HEADS):
        lg = _log_gamma(hd)
        q, k, v, gate = _ret_head_proj(h, win_ref, hd, dk, dv)
        q = _rotary(q, cos, sin)
        k = _rotary(k, cos, sin)
        v = v.astype(BF16)
        s_cur = s_ref[0, hd]
        for ci in range(t // c):
            rows = slice(ci * c, (ci + 1) * c)
            qc, kc, vc = q[rows], k[rows], v[rows]
            q_dec = (qc * qdec_ref[hd]).astype(BF16)
            k_dec = kc * kdec_ref[hd]
            scores = _dot_nt(qc.astype(BF16), kc.astype(BF16)) * dmask_ref[hd]
            lhs = jnp.concatenate([scores.astype(BF16), q_dec], axis=1)
            rhs = jnp.concatenate([vc, s_cur.astype(BF16)], axis=0)
            o = _dot(lhs, rhs)
            s_cur = math.exp(lg * c) * s_cur + _dot(k_dec.T.astype(BF16), vc)
            o = o * lax.rsqrt(jnp.mean(o * o, axis=-1, keepdims=True) + EPS)
            og_ref[rows, hd * dv:(hd + 1) * dv] = (_silu(gate[rows]) * o).astype(BF16)
        s_ref[0, hd] = s_cur
    xo_ref[0] = x + _dot(og_ref[...], wout_ref[...])


def _ret_prompt_call(x, rope, s_shape, g, win, wout, mix_layer, ret_layer):
    b, l, d = x.shape
    nh, dk, dv = s_shape
    t, c = PROMPT_TILE, RET_CHUNK
    cosr, sinr, cosb, sinb = rope
    half = cosr.shape[1]
    base_spec = pl.BlockSpec((None, 1, half), lambda i, j: (j, 0, 0))
    return pl.pallas_call(
        _ret_prompt_kernel,
        grid=(b, l // t),
        in_specs=[pl.BlockSpec((1, t, d), lambda i, j: (i, j, 0)),
                  _const_spec(cosr.shape), _const_spec(sinr.shape), base_spec, base_spec,
                  _layer_spec(g.shape, mix_layer), _layer_spec(win.shape, ret_layer),
                  _layer_spec(wout.shape, ret_layer)],
        out_specs=[pl.BlockSpec((1, t, d), lambda i, j: (i, j, 0)),
                   pl.BlockSpec((1, nh, dk, dv), lambda i, j: (i, 0, 0, 0))],
        out_shape=[jax.ShapeDtypeStruct((b, l, d), F32),
                   jax.ShapeDtypeStruct((b, nh, dk, dv), F32)],
        scratch_shapes=[pltpu.VMEM((nh, c, c), F32), pltpu.VMEM((nh, c, dk), F32),
                        pltpu.VMEM((nh, c, dk), F32), pltpu.VMEM((t, nh * dv), BF16)],
        compiler_params=_params(2),
        name="ret_prompt",
    )(x, cosr, sinr, cosb, sinb, g, win, wout)


def _ret_sample_kernel(x_ref, cos_ref, sin_ref, s0_ref, g_ref, win_ref, wout_ref,
                       xo_ref, s_ref, q_ref, k_ref, v_ref, gate_ref, og_ref, *, seg):
    dk, dv = s_ref.shape[2], s_ref.shape[3]
    s = pl.program_id(0)

    @pl.when(s == 0)
    def _():
        h = _rms(x_ref[...], g_ref[...]).astype(BF16)
        cos, sin = cos_ref[...], sin_ref[...]
        for hd in range(RET_HEADS):
            q, k, v, gate = _ret_head_proj(h, win_ref, hd, dk, dv)
            q_ref[:, hd * dk:(hd + 1) * dk] = _rotary(q, cos, sin)
            k_ref[:, hd * dk:(hd + 1) * dk] = _rotary(k, cos, sin)
            v_ref[:, hd * dv:(hd + 1) * dv] = v
            gate_ref[:, hd * dv:(hd + 1) * dv] = gate

    rows = pl.ds(pl.multiple_of(s * seg, seg), seg)
    idx = lax.broadcasted_iota(jnp.int32, (seg, 1), 0).astype(F32)
    for hd in range(RET_HEADS):
        lg = _log_gamma(hd)
        q = q_ref[rows, hd * dk:(hd + 1) * dk]
        k = k_ref[rows, hd * dk:(hd + 1) * dk]
        v = v_ref[rows, hd * dv:(hd + 1) * dv].astype(BF16)
        gate = gate_ref[rows, hd * dv:(hd + 1) * dv]
        q_dec = (q * jnp.exp(lg * (idx + 1.0))).astype(BF16)
        k_dec = (k * jnp.exp(lg * (seg - 1.0 - idx))).astype(BF16)
        scores = _dot_nt(q.astype(BF16), k.astype(BF16)) * _decay_mask(lg, seg)
        s_old = s0_ref[0, hd]
        o = _dot(scores.astype(BF16), v) + _dot(q_dec, s_old.astype(BF16))
        s_ref[0, hd] = math.exp(lg * seg) * s_old + _dot_tn(k_dec, v)
        o = o * lax.rsqrt(jnp.mean(o * o, axis=-1, keepdims=True) + EPS)
        og_ref[rows, hd * dv:(hd + 1) * dv] = _silu(gate) * o

    @pl.when(s == pl.num_programs(0) - 1)
    def _():
        xo_ref[...] = x_ref[...] + _dot(og_ref[...].astype(BF16), wout_ref[...])


def _ret_sample_call(x, cos, sin, state, g, win, wout, mix_layer, ret_layer):
    ns, seg, d = x.shape
    _, _, nh, dk, dv = state.shape
    n = ns * seg
    out, s_new = pl.pallas_call(
        functools.partial(_ret_sample_kernel, seg=seg),
        grid=(ns,),
        in_specs=[_const_spec((n, d)), _const_spec(cos.shape), _const_spec(sin.shape),
                  pl.BlockSpec((None, 1, nh, dk, dv), lambda i: (ret_layer, i, 0, 0, 0)),
                  _layer_spec(g.shape, mix_layer), _layer_spec(win.shape, ret_layer),
                  _layer_spec(wout.shape, ret_layer)],
        out_specs=[pl.BlockSpec((n, d), lambda i: (0, 0)),
                   pl.BlockSpec((1, nh, dk, dv), lambda i: (i, 0, 0, 0))],
        out_shape=[jax.ShapeDtypeStruct((n, d), F32),
                   jax.ShapeDtypeStruct((ns, nh, dk, dv), F32)],
        scratch_shapes=[pltpu.VMEM((n, nh * dk), F32), pltpu.VMEM((n, nh * dk), F32),
                        pltpu.VMEM((n, nh * dv), F32), pltpu.VMEM((n, nh * dv), F32),
                        pltpu.VMEM((n, nh * dv), F32)],
        compiler_params=_params(1),
        name="ret_sample",
    )(x.reshape(n, d), cos, sin, state, g, win, wout)
    return out.reshape(ns, seg, d), s_new


def _pool_ffn_kernel(x_ref, buf_ref, nm_ref, wp_ref, ps_ref, nf_ref, win_ref, wout_ref,
                     nfin_ref, xo_ref, bo_ref, xp_ref, *, pos0, final):
    nseg, lt, d = x_ref.shape
    gw = d // len(POOL_WINDOWS)
    j = pl.program_id(1)

    @pl.when(j == 0)
    def _():
        xp_ref[:, 0:POOL_HIST, :] = buf_ref[...]

    x = x_ref[...]
    h = _rms(x, nm_ref[...])
    xp_ref[:, POOL_HIST:, :] = h
    pos = pos0 + j * lt + lax.broadcasted_iota(jnp.int32, (1, lt, 1), 1)
    ys = []
    for gi, w in enumerate(POOL_WINDOWS):
        cols = slice(gi * gw, (gi + 1) * gw)
        acc = xp_ref[:, :, cols]
        span = 1
        while span < w:
            acc = acc + pltpu.roll(acc, span, axis=1)
            span *= 2
        cnt = jnp.minimum(pos + 1, w).astype(F32)
        dg = (acc[:, POOL_HIST:, :] / cnt - h[:, :, cols]).astype(BF16).reshape(nseg * lt, gw)
        ys.append(_dot(dg, wp_ref[gi]))
    y = jnp.concatenate(ys, axis=-1) * ps_ref[...]
    x1 = x.reshape(nseg * lt, d) + y
    x2 = _ffn(x1, nf_ref[...], win_ref, wout_ref)
    if final:
        x2 = _rms(x2, nfin_ref[...])
    xo_ref[...] = x2.reshape(nseg, lt, d)
    tail = xp_ref[:, lt:lt + POOL_HIST, :]
    bo_ref[...] = tail
    xp_ref[:, 0:POOL_HIST, :] = tail


def _pool_ffn_call(x, buf16, nm, wp, ps, nf, win, wout, nfin, *, mix_layer, pool_layer,
                   seg_block, tile, pos0, final):
    ns, l, d = x.shape
    return pl.pallas_call(
        functools.partial(_pool_ffn_kernel, pos0=pos0, final=final),
        grid=(ns // seg_block, l // tile),
        in_specs=[pl.BlockSpec((seg_block, tile, d), lambda i, j: (i, j, 0)),
                  pl.BlockSpec((seg_block, POOL_HIST, d), lambda i, j: (i, 0, 0)),
                  _layer_spec(nm.shape, mix_layer), _layer_spec(wp.shape, pool_layer),
                  _layer_spec(ps.shape, pool_layer), _layer_spec(nf.shape, mix_layer),
                  _layer_spec(win.shape, mix_layer), _layer_spec(wout.shape, mix_layer),
                  _const_spec(nfin.shape)],
        out_specs=[pl.BlockSpec((seg_block, tile, d), lambda i, j: (i, j, 0)),
                   pl.BlockSpec((seg_block, POOL_HIST, d), lambda i, j: (i, 0, 0))],
        out_shape=[jax.ShapeDtypeStruct((ns, l, d), F32),
                   jax.ShapeDtypeStruct((ns, POOL_HIST, d), F32)],
        scratch_shapes=[pltpu.VMEM((seg_block, POOL_HIST + tile, d), F32)],
        compiler_params=_params(2),
        name="pool_ffn",
    )(x, buf16, nm, wp, ps, nf, win, wout, nfin)


def _rope_angles(pos, half):
    inv = ROPE_BASE ** (-np.arange(half, dtype=np.float64) / half)
    return np.asarray(pos, np.float64)[:, None] * inv[None, :]


def _cos_sin(ang):
    return jnp.asarray(np.cos(ang), F32), jnp.asarray(np.sin(ang), F32)


def kernel(x_prompt, x_sample, state_ret, state_pool, norm_mix, w_ret_in, w_ret_out, w_pool,
           pool_scale, norm_ffn, w_ffn_in, w_ffn_out, norm_final):
    bp, lp, d = x_prompt.shape
    bs, ls, _ = x_sample.shape
    depth = norm_mix.shape[0]
    _, _, nh, dk, dv = state_ret.shape
    assert nh == RET_HEADS and state_pool.shape[2] == POOL_HIST - 1
    assert lp % PROMPT_TILE == 0 and PROMPT_TILE % RET_CHUNK == 0 and ls >= POOL_HIST
    assert depth % 2 == 0

    w_ret_in_b = w_ret_in.astype(BF16)
    w_ret_out_b = w_ret_out.astype(BF16)
    w_pool_b = w_pool.astype(BF16)
    w_ffn_in_b = w_ffn_in.astype(BF16)
    w_ffn_out_b = w_ffn_out.astype(BF16)
    norm_mix3 = norm_mix[:, None, :]
    norm_ffn3 = norm_ffn[:, None, :]
    pool_scale3 = pool_scale[:, None, :]
    nfin = norm_final[None, :]

    half = dk // 2
    cosr, sinr = _cos_sin(_rope_angles(np.arange(PROMPT_TILE), half))
    cosb, sinb = _cos_sin(_rope_angles(np.arange(0, lp, PROMPT_TILE), half)[:, None, :])
    rope_p = (cosr, sinr, cosb, sinb)
    cos_s, sin_s = _cos_sin(np.tile(_rope_angles(PAST_LEN + np.arange(ls), half), (bs, 1)))

    pool0_p = jnp.zeros((bp, POOL_HIST, d), state_pool.dtype)
    pool0_s = jnp.pad(state_pool, ((0, 0), (0, 0), (1, 0), (0, 0)))

    xp, xs = x_prompt, x_sample
    ret_p, ret_s, pool_p, pool_s = [], [], [], []
    for i in range(depth):
        jm = i // 2
        if i % 2 == 0:
            xp, sp = _ret_prompt_call(xp, rope_p, (nh, dk, dv), norm_mix3, w_ret_in_b,
                                      w_ret_out_b, i, jm)
            xs, ss = _ret_sample_call(xs, cos_s, sin_s, state_ret, norm_mix3, w_ret_in_b,
                                      w_ret_out_b, i, jm)
            ret_p.append(sp)
            ret_s.append(ss)
            xp = _ffn_call(xp.reshape(bp * lp, d), norm_ffn3, w_ffn_in_b, w_ffn_out_b, i,
                           PROMPT_TILE).reshape(bp, lp, d)
            xs = _ffn_call(xs.reshape(bs * ls, d), norm_ffn3, w_ffn_in_b, w_ffn_out_b, i,
                           bs * ls).reshape(bs, ls, d)
        else:
            kw = dict(mix_layer=i, pool_layer=jm, final=i == depth - 1)
            args = (norm_mix3, w_pool_b, pool_scale3, norm_ffn3, w_ffn_in_b, w_ffn_out_b, nfin)
            xp, bpn = _pool_ffn_call(xp, pool0_p, *args, seg_block=1, tile=PROMPT_TILE, pos0=0, **kw)
            xs, bsn = _pool_ffn_call(xs, pool0_s[jm], *args, seg_block=bs, tile=ls, pos0=PAST_LEN,
                                     **kw)
            pool_p.append(bpn[:, 1:, :])
            pool_s.append(bsn[:, 1:, :])
    return (xp, xs, jnp.stack(ret_p), jnp.stack(ret_s), jnp.stack(pool_p), jnp.stack(pool_s))
```

```python
import functools
import math

import numpy as np

import jax
import jax.numpy as jnp
from jax import lax
from jax.experimental import pallas as pl
from jax.experimental.pallas import tpu as pltpu

F32 = jnp.float32
BF16 = jnp.bfloat16

EPS = 1e-6
ROPE_BASE = 10000.0
PAST_LEN = 1024
RET_HEADS = 4
POOL_WINDOWS = (2, 4, 8, 16)
POOL_HIST = 16
PROMPT_TILE = 512
RET_CHUNK = 256
SUBLANES, LANES = 8, 128
POOL_BLOCK = 64
FFN_CHUNK = 256
VMEM_LIMIT_BYTES = 56 * 1024 * 1024


def _log_gamma(head):
    return math.log1p(-(2.0 ** (-5.0 - head)))


def _rms(x, g):
    return (x * lax.rsqrt(jnp.mean(x * x, axis=-1, keepdims=True) + EPS)) * g


def _dot(a, b):
    return jnp.dot(a, b, preferred_element_type=F32)


def _dot_nt(a, b):
    return lax.dot_general(a, b, (((1,), (1,)), ((), ())), preferred_element_type=F32)


def _dot_tn(a, b):
    return lax.dot_general(a, b, (((0,), (0,)), ((), ())), preferred_element_type=F32)


def _rotary(t, cos, sin):
    half = t.shape[-1] // 2
    t1, t2 = t[:, :half], t[:, half:]
    return jnp.concatenate([t1 * cos - t2 * sin, t1 * sin + t2 * cos], axis=-1)


def _silu(t):
    return t * jax.nn.sigmoid(t)


def _decay_mask(lg, n):
    r = lax.broadcasted_iota(jnp.int32, (n, n), 0)
    c = lax.broadcasted_iota(jnp.int32, (n, n), 1)
    diff = (r - c).astype(F32)
    return jnp.where(diff >= 0, jnp.exp(lg * jnp.maximum(diff, 0.0)), 0.0)


def _ffn(x, nf, win_ref, wout_ref):
    d_ff = wout_ref.shape[0]
    h = _rms(x, nf).astype(BF16)
    u = _dot(h, win_ref[...])
    act = (_silu(u[:, :d_ff]) * u[:, d_ff:]).astype(BF16)
    return x + _dot(act, wout_ref[...])


def _const_spec(shape):
    nd = len(shape)
    return pl.BlockSpec(shape, lambda *_: (0,) * nd, pipeline_mode=pl.Buffered(1))


def _layer_spec(stacked_shape, layer):
    nd = len(stacked_shape)
    return pl.BlockSpec((None,) + tuple(stacked_shape[1:]),
                        lambda *_: (layer,) + (0,) * (nd - 1), pipeline_mode=pl.Buffered(1))


def _params(n_axes):
    return pltpu.CompilerParams(dimension_semantics=("arbitrary",) * n_axes,
                                vmem_limit_bytes=VMEM_LIMIT_BYTES)


def _ffn_kernel(x_ref, nf_ref, win_ref, wout_ref, o_ref):
    o_ref[...] = _ffn(x_ref[...], nf_ref[...], win_ref, wout_ref)


def _ffn_call(x2d, nf, win, wout, layer, tile):
    n, d = x2d.shape
    return pl.pallas_call(
        _ffn_kernel,
        grid=(n // tile,),
        in_specs=[pl.BlockSpec((tile, d), lambda i: (i, 0)),
                  _layer_spec(nf.shape, layer), _layer_spec(win.shape, layer),
                  _layer_spec(wout.shape, layer)],
        out_specs=pl.BlockSpec((tile, d), lambda i: (i, 0)),
        out_shape=jax.ShapeDtypeStruct((n, d), F32),
        compiler_params=_params(1),
        name="ffn",
    )(x2d, nf, win, wout)


def _ret_head_proj(h, win_ref, hd, dk, dv):
    qk = RET_HEADS * dk
    vd = RET_HEADS * dv
    q = _dot(h, win_ref[:, hd * dk:(hd + 1) * dk])
    k = _dot(h, win_ref[:, qk + hd * dk:qk + (hd + 1) * dk]) * (dk ** -0.5)
    v = _dot(h, win_ref[:, 2 * qk + hd * dv:2 * qk + (hd + 1) * dv])
    gate = _dot(h, win_ref[:, 2 * qk + vd + hd * dv:2 * qk + vd + (hd + 1) * dv])
    return q, k, v, gate


def _ret_prompt_kernel(x_ref, cosr_ref, sinr_ref, cosb_ref, sinb_ref, g_ref, win_ref, wout_ref,
                       xo_ref, s_ref, dmask_ref, qdec_ref, kdec_ref, og_ref):
    t = x_ref.shape[1]
    c = dmask_ref.shape[1]
    dk, dv = s_ref.shape[2], s_ref.shape[3]
    j = pl.program_id(1)

    @pl.when((pl.program_id(0) == 0) & (j == 0))
    def _():
        row = lax.broadcasted_iota(jnp.int32, (c, dk), 0).astype(F32)
        col = lax.broadcasted_iota(jnp.int32, (dk, c), 1).astype(F32)
        for hd in range(RET_HEADS):
            lg = _log_gamma(hd)
            dmask_ref[hd] = _decay_mask(lg, c)
            qdec_ref[hd] = jnp.exp(lg * (row + 1.0))
            kdec_ref[hd] = jnp.exp(lg * (c - 1.0 - col))

    @pl.when(j == 0)
    def _():
        s_ref[...] = jnp.zeros(s_ref.shape, F32)

    x = x_ref[0]
    h = _rms(x, g_ref[...]).astype(BF16)
    cosb, sinb = cosb_ref[...], sinb_ref[...]
    cosr, sinr = cosr_ref[...], sinr_ref[...]
    cos = cosb * cosr - sinb * sinr
    sin = sinb * cosr + cosb * sinr
    for hd in range(RET_HEADS):
        lg = _log_gamma(hd)
        q, k, v, gate = _ret_head_proj(h, win_ref, hd, dk, dv)
        q = _rotary(q, cos, sin)
        k_t = _rotary(k, cos, sin).T
        v = v.astype(BF16)
        s_cur = s_ref[0, hd]
        for ci in range(t // c):
            rows = slice(ci * c, (ci + 1) * c)
            qc, kc_t, vc = q[rows], k_t[:, rows], v[rows]
            q_dec = (qc * qdec_ref[hd]).astype(BF16)
            k_dec_t = (kc_t * kdec_ref[hd]).astype(BF16)
            scores = _dot(qc.astype(BF16), kc_t.astype(BF16)) * dmask_ref[hd]
            lhs = jnp.concatenate([scores.astype(BF16), q_dec], axis=1)
            rhs = jnp.concatenate([vc, s_cur.astype(BF16)], axis=0)
            o = _dot(lhs, rhs)
            s_cur = math.exp(lg * c) * s_cur + _dot(k_dec_t, vc)
            o = o * lax.rsqrt(jnp.mean(o * o, axis=-1, keepdims=True) + EPS)
            og_ref[rows, hd * dv:(hd + 1) * dv] = (_silu(gate[rows]) * o).astype(BF16)
        s_ref[0, hd] = s_cur
    xo_ref[0] = x + _dot(og_ref[...], wout_ref[...])


def _ret_prompt_call(x, rope, s_shape, g, win, wout, mix_layer, ret_layer):
    b, l, d = x.shape
    nh, dk, dv = s_shape
    t, c = PROMPT_TILE, RET_CHUNK
    cosr, sinr, cosb, sinb = rope
    half = cosr.shape[1]
    base_spec = pl.BlockSpec((None, 1, half), lambda i, j: (j, 0, 0))
    return pl.pallas_call(
        _ret_prompt_kernel,
        grid=(b, l // t),
        in_specs=[pl.BlockSpec((1, t, d), lambda i, j: (i, j, 0)),
                  _const_spec(cosr.shape), _const_spec(sinr.shape), base_spec, base_spec,
                  _layer_spec(g.shape, mix_layer), _layer_spec(win.shape, ret_layer),
                  _layer_spec(wout.shape, ret_layer)],
        out_specs=[pl.BlockSpec((1, t, d), lambda i, j: (i, j, 0)),
                   pl.BlockSpec((1, nh, dk, dv), lambda i, j: (i, 0, 0, 0))],
        out_shape=[jax.ShapeDtypeStruct((b, l, d), F32),
                   jax.ShapeDtypeStruct((b, nh, dk, dv), F32)],
        scratch_shapes=[pltpu.VMEM((nh, c, c), F32), pltpu.VMEM((nh, c, dk), F32),
                        pltpu.VMEM((nh, dk, c), F32), pltpu.VMEM((t, nh * dv), BF16)],
        compiler_params=_params(2),
        name="ret_prompt",
    )(x, cosr, sinr, cosb, sinb, g, win, wout)


def _ret_sample_kernel(x_ref, cos_ref, sin_ref, s0_ref, g_ref, win_ref, wout_ref, *rest, seg):
    earlier = rest[:-7]
    xo_ref, s_ref, q_ref, k_ref, v_ref, gate_ref, og_ref = rest[-7:]
    if earlier:
        for layer, prev_ref in enumerate(earlier):
            s_ref[layer] = prev_ref[...]
        s_ref = s_ref.at[len(earlier)]
    dk, dv = s_ref.shape[2], s_ref.shape[3]
    s = pl.program_id(0)

    @pl.when(s == 0)
    def _():
        h = _rms(x_ref[...], g_ref[...]).astype(BF16)
        cos, sin = cos_ref[...], sin_ref[...]
        for hd in range(RET_HEADS):
            q, k, v, gate = _ret_head_proj(h, win_ref, hd, dk, dv)
            q_ref[:, hd * dk:(hd + 1) * dk] = _rotary(q, cos, sin)
            k_ref[:, hd * dk:(hd + 1) * dk] = _rotary(k, cos, sin)
            v_ref[:, hd * dv:(hd + 1) * dv] = v
            gate_ref[:, hd * dv:(hd + 1) * dv] = gate

    rows = pl.ds(pl.multiple_of(s * seg, seg), seg)
    idx = lax.broadcasted_iota(jnp.int32, (seg, 1), 0).astype(F32)
    for hd in range(RET_HEADS):
        lg = _log_gamma(hd)
        q = q_ref[rows, hd * dk:(hd + 1) * dk]
        k = k_ref[rows, hd * dk:(hd + 1) * dk]
        v = v_ref[rows, hd * dv:(hd + 1) * dv].astype(BF16)
        gate = gate_ref[rows, hd * dv:(hd + 1) * dv]
        q_dec = (q * jnp.exp(lg * (idx + 1.0))).astype(BF16)
        k_dec = (k * jnp.exp(lg * (seg - 1.0 - idx))).astype(BF16)
        scores = _dot_nt(q.astype(BF16), k.astype(BF16)) * _decay_mask(lg, seg)
        s_old = s0_ref[0, hd]
        o = _dot(scores.astype(BF16), v) + _dot(q_dec, s_old.astype(BF16))
        s_ref[0, hd] = math.exp(lg * seg) * s_old + _dot_tn(k_dec, v)
        o = o * lax.rsqrt(jnp.mean(o * o, axis=-1, keepdims=True) + EPS)
        og_ref[rows, hd * dv:(hd + 1) * dv] = _silu(gate) * o

    @pl.when(s == pl.num_programs(0) - 1)
    def _():
        xo_ref[...] = x_ref[...] + _dot(og_ref[...].astype(BF16), wout_ref[...])


def _ret_sample_call(x, cos, sin, state, g, win, wout, mix_layer, ret_layer, earlier):
    ns, seg, d = x.shape
    _, _, nh, dk, dv = state.shape
    n = ns * seg
    stream_spec = pl.BlockSpec((1, nh, dk, dv), lambda i: (i, 0, 0, 0))
    if earlier:
        n_out = len(earlier) + 1
        out_state = jax.ShapeDtypeStruct((n_out, ns, nh, dk, dv), F32)
        out_state_spec = pl.BlockSpec((n_out, 1, nh, dk, dv), lambda i: (0, i, 0, 0, 0))
    else:
        out_state = jax.ShapeDtypeStruct((ns, nh, dk, dv), F32)
        out_state_spec = stream_spec
    out, s_new = pl.pallas_call(
        functools.partial(_ret_sample_kernel, seg=seg),
        grid=(ns,),
        in_specs=[_const_spec((n, d)), _const_spec(cos.shape), _const_spec(sin.shape),
                  pl.BlockSpec((None, 1, nh, dk, dv), lambda i: (ret_layer, i, 0, 0, 0)),
                  _layer_spec(g.shape, mix_layer), _layer_spec(win.shape, ret_layer),
                  _layer_spec(wout.shape, ret_layer)] + [stream_spec] * len(earlier),
        out_specs=[pl.BlockSpec((n, d), lambda i: (0, 0)), out_state_spec],
        out_shape=[jax.ShapeDtypeStruct((n, d), F32), out_state],
        scratch_shapes=[pltpu.VMEM((n, nh * dk), F32), pltpu.VMEM((n, nh * dk), F32),
                        pltpu.VMEM((n, nh * dv), F32), pltpu.VMEM((n, nh * dv), F32),
                        pltpu.VMEM((n, nh * dv), F32)],
        compiler_params=_params(1),
        name="ret_sample",
    )(x.reshape(n, d), cos, sin, state, g, win, wout, *earlier)
    return out.reshape(ns, seg, d), s_new


def _pool_diff_pieces(x_ref, nm_ref, xp_ref, d_ref, pos_start):
    nseg, lt, d = x_ref.shape
    gw = d // len(POOL_WINDOWS)
    blk = min(lt, POOL_BLOCK)

    def piece(sg, rb, after=None):
        r0 = rb * blk
        gain = nm_ref[...]
        if after is not None:
            zero = jnp.concatenate([_zero_bits(after)[0:1]] * (d // LANES), axis=1)
            gain = pltpu.bitcast(pltpu.bitcast(gain, jnp.uint32) | zero, F32)
        h = _rms(x_ref[sg, r0:r0 + blk, :], gain)
        xp_ref[sg, POOL_HIST + r0:POOL_HIST + r0 + blk, :] = h
        pos = pos_start + r0 + lax.broadcasted_iota(jnp.int32, (blk, 1), 0)
        for gi, w in enumerate(POOL_WINDOWS):
            cols = slice(gi * gw, (gi + 1) * gw)
            acc = xp_ref[sg, r0:r0 + POOL_HIST + blk, cols]
            span = 1
            while span < w:
                acc = acc + pltpu.roll(acc, span, axis=0)
                span *= 2
            cnt = jnp.minimum(pos + 1, w).astype(F32)
            dg = acc[POOL_HIST:, :] / cnt - h[:, cols]
            d_ref[sg * lt + r0:sg * lt + r0 + blk, cols] = dg.astype(BF16)
            token = dg if gi == 0 else token + dg
        token = sum(token[:, i:i + LANES] for i in range(0, gw, LANES))
        return token.reshape(blk // SUBLANES, SUBLANES, LANES).sum(axis=0)

    return [functools.partial(piece, sg, rb) for sg in range(nseg) for rb in range(lt // blk)]


def _zero_bits(token):
    bits = pltpu.bitcast(token, jnp.uint32)
    return lax.shift_right_logical(lax.shift_right_logical(bits, jnp.uint32(16)), jnp.uint32(16))


def _tie(ref, token):
    rows = SUBLANES * 4 // jnp.dtype(ref.dtype).itemsize
    bits = pltpu.bitcast(ref[0:rows, 0:LANES], jnp.uint32)
    ref[0:rows, 0:LANES] = pltpu.bitcast(bits | _zero_bits(token), ref.dtype)


def _pool_project(x_ref, d_ref, wp_ref, ps_ref, xp_ref, bo_ref):
    nseg, lt, d = x_ref.shape
    gw = d // len(POOL_WINDOWS)
    ys = [_dot(d_ref[:, gi * gw:(gi + 1) * gw], wp_ref[gi]) for gi in range(len(POOL_WINDOWS))]
    y = jnp.concatenate(ys, axis=-1) * ps_ref[...]
    tail = xp_ref[:, lt:lt + POOL_HIST, :]
    bo_ref[...] = tail
    xp_ref[:, 0:POOL_HIST, :] = tail
    return x_ref[...].reshape(nseg * lt, d) + y


def _ffn_interleaved(x, nf, win_ref, wout_ref, h_ref, act_ref, fillers):
    d_ff = wout_ref.shape[0]
    n_chunks = d_ff // FFN_CHUNK
    assert len(fillers) < n_chunks
    hf = _rms(x, nf)
    h0 = hf.astype(BF16)
    h_ref[...] = h0
    prev = hf[-SUBLANES:, -LANES:]
    for c in range(n_chunks):
        cols = slice(c * FFN_CHUNK, (c + 1) * FFN_CHUNK)
        gate_cols = slice(d_ff + c * FFN_CHUNK, d_ff + (c + 1) * FFN_CHUNK)
        h = h_ref[...] if c else h0
        up = _dot(h, win_ref[:, gate_cols])
        act_ref[:, cols] = (_silu(_dot(h, win_ref[:, cols])) * up).astype(BF16)
        if c < len(fillers):
            _tie(h_ref, fillers[c](after=prev))
        prev = up[0:SUBLANES, 0:LANES]
    return x + _dot(act_ref[...], wout_ref[...])


def _pool_ffn_kernel(x_ref, buf_ref, nm_ref, wp_ref, ps_ref, nf_ref, win_ref, wout_ref,
                     nfin_ref, xo_ref, bo_ref, xp_ref, d_ref, h_ref, act_ref, x1a_ref, x1b_ref, *,
                     pos0, final, seq_tiles, skew):
    nseg, lt, d = x_ref.shape
    s = pl.program_id(0)
    tile = jnp.minimum(s, pl.num_programs(0) - 1 - skew)
    j = tile % seq_tiles

    @pl.when(j == 0)
    def _():
        xp_ref[:, 0:POOL_HIST, :] = buf_ref[...]

    pieces = _pool_diff_pieces(x_ref, nm_ref, xp_ref, d_ref, pos0 + j * lt)

    def project():
        return _pool_project(x_ref, d_ref, wp_ref, ps_ref, xp_ref, bo_ref)

    def ffn_out(x1, fillers):
        x2 = _ffn_interleaved(x1, nf_ref[...], win_ref, wout_ref, h_ref, act_ref, fillers)
        if final:
            x2 = _rms(x2, nfin_ref[...])
        xo_ref[...] = x2.reshape(nseg, lt, d)

    if not skew:
        for piece in pieces:
            piece()
        ffn_out(project(), [])
        return

    @pl.when(s == 0)
    def _():
        x1b_ref[...] = jnp.zeros(x1b_ref.shape, F32)

    def step(write_ref, read_ref):
        ffn_out(read_ref[...], pieces)
        write_ref[...] = project()

    @pl.when(s % 2 == 0)
    def _():
        step(x1a_ref, x1b_ref)

    @pl.when(s % 2 == 1)
    def _():
        step(x1b_ref, x1a_ref)


def _pool_ffn_call(x, buf16, nm, wp, ps, nf, win, wout, nfin, *, mix_layer, pool_layer,
                   seg_block, tile, pos0, final, skew):
    ns, l, d = x.shape
    seq_tiles = l // tile
    n_tiles = (ns // seg_block) * seq_tiles

    def mixer_tile(s):
        t = jnp.minimum(s, n_tiles - 1)
        return t // seq_tiles, t % seq_tiles

    def ffn_tile(s):
        t = jnp.maximum(s - skew, 0)
        return t // seq_tiles, t % seq_tiles

    rows = seg_block * tile
    x1_shape = (rows, d) if skew else (8, 128)
    return pl.pallas_call(
        functools.partial(_pool_ffn_kernel, pos0=pos0, final=final, seq_tiles=seq_tiles,
                          skew=skew),
        grid=(n_tiles + skew,),
        in_specs=[pl.BlockSpec((seg_block, tile, d), lambda s: (*mixer_tile(s), 0)),
                  pl.BlockSpec((seg_block, POOL_HIST, d), lambda s: (mixer_tile(s)[0], 0, 0)),
                  _layer_spec(nm.shape, mix_layer), _layer_spec(wp.shape, pool_layer),
                  _layer_spec(ps.shape, pool_layer), _layer_spec(nf.shape, mix_layer),
                  _layer_spec(win.shape, mix_layer), _layer_spec(wout.shape, mix_layer),
                  _const_spec(nfin.shape)],
        out_specs=[pl.BlockSpec((seg_block, tile, d), lambda s: (*ffn_tile(s), 0)),
                   pl.BlockSpec((seg_block, POOL_HIST, d), lambda s: (mixer_tile(s)[0], 0, 0))],
        out_shape=[jax.ShapeDtypeStruct((ns, l, d), F32),
                   jax.ShapeDtypeStruct((ns, POOL_HIST, d), F32)],
        scratch_shapes=[pltpu.VMEM((seg_block, POOL_HIST + tile, d), F32),
                        pltpu.VMEM((rows, d), BF16), pltpu.VMEM((rows, d), BF16),
                        pltpu.VMEM((rows, wout.shape[1]), BF16),
                        pltpu.VMEM(x1_shape, F32), pltpu.VMEM(x1_shape, F32)],
        compiler_params=_params(1),
        name="pool_ffn",
    )(x, buf16, nm, wp, ps, nf, win, wout, nfin)


def _rope_angles(pos, half):
    inv = ROPE_BASE ** (-np.arange(half, dtype=np.float64) / half)
    return np.asarray(pos, np.float64)[:, None] * inv[None, :]


def _cos_sin(ang):
    return jnp.asarray(np.cos(ang), F32), jnp.asarray(np.sin(ang), F32)


def kernel(x_prompt, x_sample, state_ret, state_pool, norm_mix, w_ret_in, w_ret_out, w_pool,
           pool_scale, norm_ffn, w_ffn_in, w_ffn_out, norm_final):
    bp, lp, d = x_prompt.shape
    bs, ls, _ = x_sample.shape
    depth = norm_mix.shape[0]
    _, _, nh, dk, dv = state_ret.shape
    assert nh == RET_HEADS and state_pool.shape[2] == POOL_HIST - 1
    assert lp % PROMPT_TILE == 0 and PROMPT_TILE % RET_CHUNK == 0 and ls >= POOL_HIST
    assert depth % 2 == 0

    w_ret_in_b = w_ret_in.astype(BF16)
    w_ret_out_b = w_ret_out.astype(BF16)
    w_pool_b = w_pool.astype(BF16)
    w_ffn_in_b = w_ffn_in.astype(BF16)
    w_ffn_out_b = w_ffn_out.astype(BF16)
    norm_mix3 = norm_mix[:, None, :]
    norm_ffn3 = norm_ffn[:, None, :]
    pool_scale3 = pool_scale[:, None, :]
    nfin = norm_final[None, :]

    half = dk // 2
    cosr, sinr = _cos_sin(_rope_angles(np.arange(PROMPT_TILE), half))
    cosb, sinb = _cos_sin(_rope_angles(np.arange(0, lp, PROMPT_TILE), half)[:, None, :])
    rope_p = (cosr, sinr, cosb, sinb)
    cos_s, sin_s = _cos_sin(np.tile(_rope_angles(PAST_LEN + np.arange(ls), half), (bs, 1)))

    pool0_p = jnp.zeros((bp, POOL_HIST, d), state_pool.dtype)
    pool0_s = jnp.pad(state_pool, ((0, 0), (0, 0), (1, 0), (0, 0)))

    xp, xs = x_prompt, x_sample
    ret_p, pool_p, pool_s = [], [], []
    n_ret = state_ret.shape[0]
    ret_s = []
    for i in range(depth):
        jm = i // 2
        if i % 2 == 0:
            xp, sp = _ret_prompt_call(xp, rope_p, (nh, dk, dv), norm_mix3, w_ret_in_b,
                                      w_ret_out_b, i, jm)
            earlier = list(ret_s) if jm == n_ret - 1 else []
            xs, ss = _ret_sample_call(xs, cos_s, sin_s, state_ret, norm_mix3, w_ret_in_b,
                                      w_ret_out_b, i, jm, earlier)
            ret_s.append(ss)
            ret_p.append(sp)
            xp = _ffn_call(xp.reshape(bp * lp, d), norm_ffn3, w_ffn_in_b, w_ffn_out_b, i,
                           PROMPT_TILE).reshape(bp, lp, d)
            xs = _ffn_call(xs.reshape(bs * ls, d), norm_ffn3, w_ffn_in_b, w_ffn_out_b, i,
                           bs * ls).reshape(bs, ls, d)
        else:
            kw = dict(mix_layer=i, pool_layer=jm, final=i == depth - 1)
            args = (norm_mix3, w_pool_b, pool_scale3, norm_ffn3, w_ffn_in_b, w_ffn_out_b, nfin)
            xp, bpn = _pool_ffn_call(xp, pool0_p, *args, seg_block=1, tile=PROMPT_TILE, pos0=0,
                                     skew=1, **kw)
            xs, bsn = _pool_ffn_call(xs, pool0_s[jm], *args, seg_block=bs, tile=ls, pos0=PAST_LEN,
                                     skew=0, **kw)
            pool_p.append(bpn[:, 1:, :])
            pool_s.append(bsn[:, 1:, :])
    ret_s_stack = ret_s[-1] if n_ret > 1 else ret_s[0][None]
    return (xp, xs, jnp.stack(ret_p), ret_s_stack, jnp.stack(pool_p), jnp.stack(pool_s))
```

```python
import functools
import math

import numpy as np

import jax
import jax.numpy as jnp
from jax import lax
from jax.experimental import pallas as pl
from jax.experimental.pallas import tpu as pltpu

F32 = jnp.float32
BF16 = jnp.bfloat16

EPS = 1e-6
ROPE_BASE = 10000.0
PAST_LEN = 1024
RET_HEADS = 4
POOL_WINDOWS = (2, 4, 8, 16)
POOL_HIST = 16
PROMPT_TILE = 512
RET_CHUNK = 256
SUBLANES, LANES = 8, 128
POOL_BLOCK = 64
FFN_CHUNK = 256
VMEM_LIMIT_BYTES = 56 * 1024 * 1024


def _log_gamma(head):
    return math.log1p(-(2.0 ** (-5.0 - head)))


def _rms(x, g):
    return (x * lax.rsqrt(jnp.mean(x * x, axis=-1, keepdims=True) + EPS)) * g


def _dot(a, b):
    return jnp.dot(a, b, preferred_element_type=F32)


def _dot_nt(a, b):
    return lax.dot_general(a, b, (((1,), (1,)), ((), ())), preferred_element_type=F32)


def _dot_tn(a, b):
    return lax.dot_general(a, b, (((0,), (0,)), ((), ())), preferred_element_type=F32)


def _rotary(t, cos, sin):
    half = t.shape[-1] // 2
    t1, t2 = t[:, :half], t[:, half:]
    return jnp.concatenate([t1 * cos - t2 * sin, t1 * sin + t2 * cos], axis=-1)


def _silu(t):
    return t * jax.nn.sigmoid(t)


def _decay_mask(lg, n):
    r = lax.broadcasted_iota(jnp.int32, (n, n), 0)
    c = lax.broadcasted_iota(jnp.int32, (n, n), 1)
    diff = (r - c).astype(F32)
    return jnp.where(diff >= 0, jnp.exp(lg * jnp.maximum(diff, 0.0)), 0.0)


def _ffn(x, nf, win_ref, wout_ref):
    d_ff = wout_ref.shape[0]
    h = _rms(x, nf).astype(BF16)
    u = _dot(h, win_ref[...])
    act = (_silu(u[:, :d_ff]) * u[:, d_ff:]).astype(BF16)
    return x + _dot(act, wout_ref[...])


def _const_spec(shape):
    nd = len(shape)
    return pl.BlockSpec(shape, lambda *_: (0,) * nd, pipeline_mode=pl.Buffered(1))


def _layer_spec(stacked_shape, layer):
    nd = len(stacked_shape)
    return pl.BlockSpec((None,) + tuple(stacked_shape[1:]),
                        lambda *_: (layer,) + (0,) * (nd - 1), pipeline_mode=pl.Buffered(1))


def _params(n_axes):
    return pltpu.CompilerParams(dimension_semantics=("arbitrary",) * n_axes,
                                vmem_limit_bytes=VMEM_LIMIT_BYTES)


def _ffn_kernel(x_ref, nf_ref, win_ref, wout_ref, o_ref):
    o_ref[...] = _ffn(x_ref[...], nf_ref[...], win_ref, wout_ref)


def _ffn_call(x2d, nf, win, wout, layer, tile):
    n, d = x2d.shape
    return pl.pallas_call(
        _ffn_kernel,
        grid=(n // tile,),
        in_specs=[pl.BlockSpec((tile, d), lambda i: (i, 0)),
                  _layer_spec(nf.shape, layer), _layer_spec(win.shape, layer),
                  _layer_spec(wout.shape, layer)],
        out_specs=pl.BlockSpec((tile, d), lambda i: (i, 0)),
        out_shape=jax.ShapeDtypeStruct((n, d), F32),
        compiler_params=_params(1),
        name="ffn",
    )(x2d, nf, win, wout)


def _ret_head_proj(h, win_ref, hd, dk, dv):
    qk = RET_HEADS * dk
    vd = RET_HEADS * dv
    q = _dot(h, win_ref[:, hd * dk:(hd + 1) * dk])
    k = _dot(h, win_ref[:, qk + hd * dk:qk + (hd + 1) * dk]) * (dk ** -0.5)
    v = _dot(h, win_ref[:, 2 * qk + hd * dv:2 * qk + (hd + 1) * dv])
    gate = _dot(h, win_ref[:, 2 * qk + vd + hd * dv:2 * qk + vd + (hd + 1) * dv])
    return q, k, v, gate


def _ret_prompt_kernel(x_ref, cosr_ref, sinr_ref, cosb_ref, sinb_ref, g_ref, win_ref, wout_ref,
                       xo_ref, s_ref, dmask_ref, qdec_ref, kdec_ref, og_ref):
    t = x_ref.shape[1]
    c = dmask_ref.shape[1]
    dk, dv = s_ref.shape[2], s_ref.shape[3]
    j = pl.program_id(1)

    @pl.when((pl.program_id(0) == 0) & (j == 0))
    def _():
        row = lax.broadcasted_iota(jnp.int32, (c, dk), 0).astype(F32)
        col = lax.broadcasted_iota(jnp.int32, (dk, c), 1).astype(F32)
        for hd in range(RET_HEADS):
            lg = _log_gamma(hd)
            dmask_ref[hd] = _decay_mask(lg, c)
            qdec_ref[hd] = jnp.exp(lg * (row + 1.0))
            kdec_ref[hd] = jnp.exp(lg * (c - 1.0 - col))

    @pl.when(j == 0)
    def _():
        s_ref[...] = jnp.zeros(s_ref.shape, F32)

    x = x_ref[0]
    h = _rms(x, g_ref[...]).astype(BF16)
    cosb, sinb = cosb_ref[...], sinb_ref[...]
    cosr, sinr = cosr_ref[...], sinr_ref[...]
    cos = cosb * cosr - sinb * sinr
    sin = sinb * cosr + cosb * sinr
    qk, vd = RET_HEADS * dk, RET_HEADS * dv
    n_chunks = t // c
    proj = [dict() for _ in range(RET_HEADS)]
    out = [x]

    def proj_steps(hd):
        def q_step():
            proj[hd]["q"] = _rotary(_dot(h, win_ref[:, hd * dk:(hd + 1) * dk]), cos, sin)

        def k_step():
            k = _dot(h, win_ref[:, qk + hd * dk:qk + (hd + 1) * dk]) * (dk ** -0.5)
            proj[hd]["k_t"] = _rotary(k, cos, sin).T

        def v_step():
            lo = 2 * qk + hd * dv
            proj[hd]["v"] = _dot(h, win_ref[:, lo:lo + dv]).astype(BF16)

        def gate_step():
            lo = 2 * qk + vd + hd * dv
            proj[hd]["gate"] = _dot(h, win_ref[:, lo:lo + dv])

        return [q_step, k_step, v_step, gate_step]

    def out_step(hd):
        def step():
            out[0] = out[0] + _dot(og_ref[:, hd * dv:(hd + 1) * dv],
                                   wout_ref[hd * dv:(hd + 1) * dv, :])
        return step

    def core(hd, fillers):
        lg = _log_gamma(hd)
        q, k_t, v, gate = (proj[hd][name] for name in ("q", "k_t", "v", "gate"))
        chunks = [slice(ci * c, (ci + 1) * c) for ci in range(n_chunks)]
        scores = [_dot(q[rows].astype(BF16), k_t[:, rows].astype(BF16)) for rows in chunks]
        fillers.pop(0)()
        s_cur = s_ref[0, hd]
        for rows, sc in zip(chunks, scores):
            vc = v[rows]
            q_dec = (q[rows] * qdec_ref[hd]).astype(BF16)
            k_dec_t = (k_t[:, rows] * kdec_ref[hd]).astype(BF16)
            lhs = jnp.concatenate([(sc * dmask_ref[hd]).astype(BF16), q_dec], axis=1)
            rhs = jnp.concatenate([vc, s_cur.astype(BF16)], axis=0)
            o = _dot(lhs, rhs)
            s_cur = math.exp(lg * c) * s_cur + _dot(k_dec_t, vc)
            if fillers:
                fillers.pop(0)()
            o = o * lax.rsqrt(jnp.mean(o * o, axis=-1, keepdims=True) + EPS)
            og_ref[rows, hd * dv:(hd + 1) * dv] = (_silu(gate[rows]) * o).astype(BF16)
        s_ref[0, hd] = s_cur
        for filler in fillers:
            filler()

    for step in proj_steps(0):
        step()
    for hd in range(RET_HEADS):
        if hd + 1 < RET_HEADS:
            fillers = proj_steps(hd + 1)
        else:
            fillers = [out_step(i) for i in range(hd)]
        core(hd, fillers)
    out_step(RET_HEADS - 1)()
    xo_ref[0] = out[0]


def _ret_prompt_call(x, rope, s_shape, g, win, wout, mix_layer, ret_layer):
    b, l, d = x.shape
    nh, dk, dv = s_shape
    t, c = PROMPT_TILE, RET_CHUNK
    cosr, sinr, cosb, sinb = rope
    half = cosr.shape[1]
    base_spec = pl.BlockSpec((None, 1, half), lambda i, j: (j, 0, 0))
    return pl.pallas_call(
        _ret_prompt_kernel,
        grid=(b, l // t),
        in_specs=[pl.BlockSpec((1, t, d), lambda i, j: (i, j, 0)),
                  _const_spec(cosr.shape), _const_spec(sinr.shape), base_spec, base_spec,
                  _layer_spec(g.shape, mix_layer), _layer_spec(win.shape, ret_layer),
                  _layer_spec(wout.shape, ret_layer)],
        out_specs=[pl.BlockSpec((1, t, d), lambda i, j: (i, j, 0)),
                   pl.BlockSpec((1, nh, dk, dv), lambda i, j: (i, 0, 0, 0))],
        out_shape=[jax.ShapeDtypeStruct((b, l, d), F32),
                   jax.ShapeDtypeStruct((b, nh, dk, dv), F32)],
        scratch_shapes=[pltpu.VMEM((nh, c, c), F32), pltpu.VMEM((nh, c, dk), F32),
                        pltpu.VMEM((nh, dk, c), F32), pltpu.VMEM((t, nh * dv), BF16)],
        compiler_params=_params(2),
        name="ret_prompt",
    )(x, cosr, sinr, cosb, sinb, g, win, wout)


def _ret_sample_kernel(x_ref, cos_ref, sin_ref, s0_ref, g_ref, win_ref, wout_ref, *rest, seg):
    earlier = rest[:-7]
    xo_ref, s_ref, q_ref, k_ref, v_ref, gate_ref, og_ref = rest[-7:]
    if earlier:
        for layer, prev_ref in enumerate(earlier):
            s_ref[layer] = prev_ref[...]
        s_ref = s_ref.at[len(earlier)]
    dk, dv = s_ref.shape[2], s_ref.shape[3]
    s = pl.program_id(0)

    @pl.when(s == 0)
    def _():
        h = _rms(x_ref[...], g_ref[...]).astype(BF16)
        cos, sin = cos_ref[...], sin_ref[...]
        for hd in range(RET_HEADS):
            q, k, v, gate = _ret_head_proj(h, win_ref, hd, dk, dv)
            q_ref[:, hd * dk:(hd + 1) * dk] = _rotary(q, cos, sin)
            k_ref[:, hd * dk:(hd + 1) * dk] = _rotary(k, cos, sin)
            v_ref[:, hd * dv:(hd + 1) * dv] = v
            gate_ref[:, hd * dv:(hd + 1) * dv] = gate

    rows = pl.ds(pl.multiple_of(s * seg, seg), seg)
    idx = lax.broadcasted_iota(jnp.int32, (seg, 1), 0).astype(F32)
    for hd in range(RET_HEADS):
        lg = _log_gamma(hd)
        q = q_ref[rows, hd * dk:(hd + 1) * dk]
        k = k_ref[rows, hd * dk:(hd + 1) * dk]
        v = v_ref[rows, hd * dv:(hd + 1) * dv].astype(BF16)
        gate = gate_ref[rows, hd * dv:(hd + 1) * dv]
        q_dec = (q * jnp.exp(lg * (idx + 1.0))).astype(BF16)
        k_dec = (k * jnp.exp(lg * (seg - 1.0 - idx))).astype(BF16)
        scores = _dot_nt(q.astype(BF16), k.astype(BF16)) * _decay_mask(lg, seg)
        s_old = s0_ref[0, hd]
        o = _dot(scores.astype(BF16), v) + _dot(q_dec, s_old.astype(BF16))
        s_ref[0, hd] = math.exp(lg * seg) * s_old + _dot_tn(k_dec, v)
        o = o * lax.rsqrt(jnp.mean(o * o, axis=-1, keepdims=True) + EPS)
        og_ref[rows, hd * dv:(hd + 1) * dv] = _silu(gate) * o

    @pl.when(s == pl.num_programs(0) - 1)
    def _():
        xo_ref[...] = x_ref[...] + _dot(og_ref[...].astype(BF16), wout_ref[...])


def _ret_sample_call(x, cos, sin, state, g, win, wout, mix_layer, ret_layer, earlier):
    ns, seg, d = x.shape
    _, _, nh, dk, dv = state.shape
    n = ns * seg
    stream_spec = pl.BlockSpec((1, nh, dk, dv), lambda i: (i, 0, 0, 0))
    if earlier:
        n_out = len(earlier) + 1
        out_state = jax.ShapeDtypeStruct((n_out, ns, nh, dk, dv), F32)
        out_state_spec = pl.BlockSpec((n_out, 1, nh, dk, dv), lambda i: (0, i, 0, 0, 0))
    else:
        out_state = jax.ShapeDtypeStruct((ns, nh, dk, dv), F32)
        out_state_spec = stream_spec
    out, s_new = pl.pallas_call(
        functools.partial(_ret_sample_kernel, seg=seg),
        grid=(ns,),
        in_specs=[_const_spec((n, d)), _const_spec(cos.shape), _const_spec(sin.shape),
                  pl.BlockSpec((None, 1, nh, dk, dv), lambda i: (ret_layer, i, 0, 0, 0)),
                  _layer_spec(g.shape, mix_layer), _layer_spec(win.shape, ret_layer),
                  _layer_spec(wout.shape, ret_layer)] + [stream_spec] * len(earlier),
        out_specs=[pl.BlockSpec((n, d), lambda i: (0, 0)), out_state_spec],
        out_shape=[jax.ShapeDtypeStruct((n, d), F32), out_state],
        scratch_shapes=[pltpu.VMEM((n, nh * dk), F32), pltpu.VMEM((n, nh * dk), F32),
                        pltpu.VMEM((n, nh * dv), F32), pltpu.VMEM((n, nh * dv), F32),
                        pltpu.VMEM((n, nh * dv), F32)],
        compiler_params=_params(1),
        name="ret_sample",
    )(x.reshape(n, d), cos, sin, state, g, win, wout, *earlier)
    return out.reshape(ns, seg, d), s_new


def _pool_diff_pieces(x_ref, nm_ref, xp_ref, d_ref, pos_start):
    nseg, lt, d = x_ref.shape
    gw = d // len(POOL_WINDOWS)
    blk = min(lt, POOL_BLOCK)

    def piece(sg, rb, after=None):
        r0 = rb * blk
        gain = nm_ref[...]
        if after is not None:
            known = jnp.concatenate([_known(after)[0:1]] * (d // LANES), axis=1)
            gain = jnp.where(known, gain, 0.0)
        h = _rms(x_ref[sg, r0:r0 + blk, :], gain)
        xp_ref[sg, POOL_HIST + r0:POOL_HIST + r0 + blk, :] = h
        pos = pos_start + r0 + lax.broadcasted_iota(jnp.int32, (blk, 1), 0)
        for gi, w in enumerate(POOL_WINDOWS):
            cols = slice(gi * gw, (gi + 1) * gw)
            acc = xp_ref[sg, r0:r0 + POOL_HIST + blk, cols]
            span = 1
            while span < w:
                acc = acc + pltpu.roll(acc, span, axis=0)
                span *= 2
            cnt = jnp.minimum(pos + 1, w).astype(F32)
            dg = acc[POOL_HIST:, :] / cnt - h[:, cols]
            d_ref[sg * lt + r0:sg * lt + r0 + blk, cols] = dg.astype(BF16)
            token = dg if gi == 0 else token + dg
        token = sum(token[:, i:i + LANES] for i in range(0, gw, LANES))
        return token.reshape(blk // SUBLANES, SUBLANES, LANES).sum(axis=0)

    return [functools.partial(piece, sg, rb) for sg in range(nseg) for rb in range(lt // blk)]


def _known(token):
    bits = pltpu.bitcast(token, jnp.uint32)
    bits = lax.shift_right_logical(lax.shift_right_logical(bits, jnp.uint32(16)), jnp.uint32(16))
    return bits == 0


def _tie(ref, token):
    rows = SUBLANES * 4 // jnp.dtype(ref.dtype).itemsize
    known = jnp.concatenate([_known(token)] * (rows // SUBLANES), axis=0)
    tile = ref[-rows:, -LANES:].astype(F32)
    ref[-rows:, -LANES:] = jnp.where(known, tile, 0.0).astype(ref.dtype)


def _pool_project(x_ref, d_ref, wp_ref, ps_ref, xp_ref, bo_ref):
    nseg, lt, d = x_ref.shape
    gw = d // len(POOL_WINDOWS)
    ys = [_dot(d_ref[:, gi * gw:(gi + 1) * gw], wp_ref[gi]) for gi in range(len(POOL_WINDOWS))]
    y = jnp.concatenate(ys, axis=-1) * ps_ref[...]
    tail = xp_ref[:, lt:lt + POOL_HIST, :]
    bo_ref[...] = tail
    xp_ref[:, 0:POOL_HIST, :] = tail
    return x_ref[...].reshape(nseg * lt, d) + y


def _ffn_interleaved(x, nf, win_ref, wout_ref, h_ref, act_ref, fillers):
    d_ff = wout_ref.shape[0]
    n_chunks = d_ff // FFN_CHUNK
    assert len(fillers) < n_chunks
    hf = _rms(x, nf)
    h0 = hf.astype(BF16)
    h_ref[...] = h0
    prev = hf[-SUBLANES:, -LANES:]
    for c in range(n_chunks):
        cols = slice(c * FFN_CHUNK, (c + 1) * FFN_CHUNK)
        gate_cols = slice(d_ff + c * FFN_CHUNK, d_ff + (c + 1) * FFN_CHUNK)
        h = h_ref[...] if c else h0
        up = _dot(h, win_ref[:, gate_cols])
        act_ref[:, cols] = (_silu(_dot(h, win_ref[:, cols])) * up).astype(BF16)
        if c < len(fillers):
            _tie(h_ref, fillers[c](after=prev))
        prev = up[0:SUBLANES, 0:LANES]
    return x + _dot(act_ref[...], wout_ref[...])


def _pool_ffn_kernel(x_ref, buf_ref, nm_ref, wp_ref, ps_ref, nf_ref, win_ref, wout_ref,
                     nfin_ref, xo_ref, bo_ref, xp_ref, d_ref, h_ref, act_ref, x1a_ref, x1b_ref, *,
                     pos0, final, seq_tiles, skew):
    nseg, lt, d = x_ref.shape
    s = pl.program_id(0)
    tile = jnp.minimum(s, pl.num_programs(0) - 1 - skew)
    j = tile % seq_tiles

    @pl.when(j == 0)
    def _():
        xp_ref[:, 0:POOL_HIST, :] = buf_ref[...]

    pieces = _pool_diff_pieces(x_ref, nm_ref, xp_ref, d_ref, pos0 + j * lt)

    def project():
        return _pool_project(x_ref, d_ref, wp_ref, ps_ref, xp_ref, bo_ref)

    def ffn_out(x1, fillers):
        x2 = _ffn_interleaved(x1, nf_ref[...], win_ref, wout_ref, h_ref, act_ref, fillers)
        if final:
            x2 = _rms(x2, nfin_ref[...])
        xo_ref[...] = x2.reshape(nseg, lt, d)

    if not skew:
        for piece in pieces:
            piece()
        ffn_out(project(), [])
        return

    @pl.when(s == 0)
    def _():
        x1b_ref[...] = jnp.zeros(x1b_ref.shape, F32)

    def step(write_ref, read_ref):
        ffn_out(read_ref[...], pieces)
        write_ref[...] = project()

    @pl.when(s % 2 == 0)
    def _():
        step(x1a_ref, x1b_ref)

    @pl.when(s % 2 == 1)
    def _():
        step(x1b_ref, x1a_ref)


def _pool_ffn_call(x, buf16, nm, wp, ps, nf, win, wout, nfin, *, mix_layer, pool_layer,
                   seg_block, tile, pos0, final, skew):
    ns, l, d = x.shape
    seq_tiles = l // tile
    n_tiles = (ns // seg_block) * seq_tiles

    def mixer_tile(s):
        t = jnp.minimum(s, n_tiles - 1)
        return t // seq_tiles, t % seq_tiles

    def ffn_tile(s):
        t = jnp.maximum(s - skew, 0)
        return t // seq_tiles, t % seq_tiles

    rows = seg_block * tile
    x1_shape = (rows, d) if skew else (8, 128)
    return pl.pallas_call(
        functools.partial(_pool_ffn_kernel, pos0=pos0, final=final, seq_tiles=seq_tiles,
                          skew=skew),
        grid=(n_tiles + skew,),
        in_specs=[pl.BlockSpec((seg_block, tile, d), lambda s: (*mixer_tile(s), 0)),
                  pl.BlockSpec((seg_block, POOL_HIST, d), lambda s: (mixer_tile(s)[0], 0, 0)),
                  _layer_spec(nm.shape, mix_layer), _layer_spec(wp.shape, pool_layer),
                  _layer_spec(ps.shape, pool_layer), _layer_spec(nf.shape, mix_layer),
                  _layer_spec(win.shape, mix_layer), _layer_spec(wout.shape, mix_layer),
                  _const_spec(nfin.shape)],
        out_specs=[pl.BlockSpec((seg_block, tile, d), lambda s: (*ffn_tile(s), 0)),
                   pl.BlockSpec((seg_block, POOL_HIST, d), lambda s: (mixer_tile(s)[0], 0, 0))],
        out_shape=[jax.ShapeDtypeStruct((ns, l, d), F32),
                   jax.ShapeDtypeStruct((ns, POOL_HIST, d), F32)],
        scratch_shapes=[pltpu.VMEM((seg_block, POOL_HIST + tile, d), F32),
                        pltpu.VMEM((rows, d), BF16), pltpu.VMEM((rows, d), BF16),
                        pltpu.VMEM((rows, wout.shape[1]), BF16),
                        pltpu.VMEM(x1_shape, F32), pltpu.VMEM(x1_shape, F32)],
        compiler_params=_params(1),
        name="pool_ffn",
    )(x, buf16, nm, wp, ps, nf, win, wout, nfin)


def _rope_angles(pos, half):
    inv = ROPE_BASE ** (-np.arange(half, dtype=np.float64) / half)
    return np.asarray(pos, np.float64)[:, None] * inv[None, :]


def _cos_sin(ang):
    return jnp.asarray(np.cos(ang), F32), jnp.asarray(np.sin(ang), F32)


def kernel(x_prompt, x_sample, state_ret, state_pool, norm_mix, w_ret_in, w_ret_out, w_pool,
           pool_scale, norm_ffn, w_ffn_in, w_ffn_out, norm_final):
    bp, lp, d = x_prompt.shape
    bs, ls, _ = x_sample.shape
    depth = norm_mix.shape[0]
    _, _, nh, dk, dv = state_ret.shape
    assert nh == RET_HEADS and state_pool.shape[2] == POOL_HIST - 1
    assert lp % PROMPT_TILE == 0 and PROMPT_TILE % RET_CHUNK == 0 and ls >= POOL_HIST
    assert depth % 2 == 0

    w_ret_in_b = w_ret_in.astype(BF16)
    w_ret_out_b = w_ret_out.astype(BF16)
    w_pool_b = w_pool.astype(BF16)
    w_ffn_in_b = w_ffn_in.astype(BF16)
    w_ffn_out_b = w_ffn_out.astype(BF16)
    norm_mix3 = norm_mix[:, None, :]
    norm_ffn3 = norm_ffn[:, None, :]
    pool_scale3 = pool_scale[:, None, :]
    nfin = norm_final[None, :]

    half = dk // 2
    cosr, sinr = _cos_sin(_rope_angles(np.arange(PROMPT_TILE), half))
    cosb, sinb = _cos_sin(_rope_angles(np.arange(0, lp, PROMPT_TILE), half)[:, None, :])
    rope_p = (cosr, sinr, cosb, sinb)
    cos_s, sin_s = _cos_sin(np.tile(_rope_angles(PAST_LEN + np.arange(ls), half), (bs, 1)))

    pool0_p = jnp.zeros((bp, POOL_HIST, d), state_pool.dtype)
    pool0_s = jnp.pad(state_pool, ((0, 0), (0, 0), (1, 0), (0, 0)))

    xp, xs = x_prompt, x_sample
    ret_p, pool_p, pool_s = [], [], []
    n_ret = state_ret.shape[0]
    ret_s = []
    for i in range(depth):
        jm = i // 2
        if i % 2 == 0:
            xp, sp = _ret_prompt_call(xp, rope_p, (nh, dk, dv), norm_mix3, w_ret_in_b,
                                      w_ret_out_b, i, jm)
            earlier = list(ret_s) if jm == n_ret - 1 else []
            xs, ss = _ret_sample_call(xs, cos_s, sin_s, state_ret, norm_mix3, w_ret_in_b,
                                      w_ret_out_b, i, jm, earlier)
            ret_s.append(ss)
            ret_p.append(sp)
            xp = _ffn_call(xp.reshape(bp * lp, d), norm_ffn3, w_ffn_in_b, w_ffn_out_b, i,
                           PROMPT_TILE).reshape(bp, lp, d)
            xs = _ffn_call(xs.reshape(bs * ls, d), norm_ffn3, w_ffn_in_b, w_ffn_out_b, i,
                           bs * ls).reshape(bs, ls, d)
        else:
            kw = dict(mix_layer=i, pool_layer=jm, final=i == depth - 1)
            args = (norm_mix3, w_pool_b, pool_scale3, norm_ffn3, w_ffn_in_b, w_ffn_out_b, nfin)
            xp, bpn = _pool_ffn_call(xp, pool0_p, *args, seg_block=1, tile=PROMPT_TILE, pos0=0,
                                     skew=1, **kw)
            xs, bsn = _pool_ffn_call(xs, pool0_s[jm], *args, seg_block=bs, tile=ls, pos0=PAST_LEN,
                                     skew=0, **kw)
            pool_p.append(bpn[:, 1:, :])
            pool_s.append(bsn[:, 1:, :])
    ret_s_stack = ret_s[-1] if n_ret > 1 else ret_s[0][None]
    return (xp, xs, jnp.stack(ret_p), ret_s_stack, jnp.stack(pool_p), jnp.stack(pool_s))
```

```python
import functools
import math

import numpy as np

import jax
import jax.numpy as jnp
from jax import lax
from jax.experimental import pallas as pl
from jax.experimental.pallas import tpu as pltpu

F32 = jnp.float32
BF16 = jnp.bfloat16

EPS = 1e-6
ROPE_BASE = 10000.0
PAST_LEN = 1024
RET_HEADS = 4
POOL_WINDOWS = (2, 4, 8, 16)
POOL_HIST = 16
PROMPT_TILE = 512
RET_CHUNK = 256
SUBLANES, LANES = 8, 128
POOL_BLOCK = 64
FFN_CHUNK = 256
VMEM_LIMIT_BYTES = 56 * 1024 * 1024


def _log_gamma(head):
    return math.log1p(-(2.0 ** (-5.0 - head)))


def _rms(x, g):
    return (x * lax.rsqrt(jnp.mean(x * x, axis=-1, keepdims=True) + EPS)) * g


def _dot(a, b):
    return jnp.dot(a, b, preferred_element_type=F32)


def _dot_nt(a, b):
    return lax.dot_general(a, b, (((1,), (1,)), ((), ())), preferred_element_type=F32)


def _dot_tn(a, b):
    return lax.dot_general(a, b, (((0,), (0,)), ((), ())), preferred_element_type=F32)


def _rotary(t, cos, sin):
    half = t.shape[-1] // 2
    t1, t2 = t[:, :half], t[:, half:]
    return jnp.concatenate([t1 * cos - t2 * sin, t1 * sin + t2 * cos], axis=-1)


def _silu(t):
    return t * jax.nn.sigmoid(t)


def _decay_mask(lg, n):
    r = lax.broadcasted_iota(jnp.int32, (n, n), 0)
    c = lax.broadcasted_iota(jnp.int32, (n, n), 1)
    diff = (r - c).astype(F32)
    return jnp.where(diff >= 0, jnp.exp(lg * jnp.maximum(diff, 0.0)), 0.0)


def _ffn(x, nf, win_ref, wout_ref):
    d_ff = wout_ref.shape[0]
    h = _rms(x, nf).astype(BF16)
    u = _dot(h, win_ref[...])
    act = (_silu(u[:, :d_ff]) * u[:, d_ff:]).astype(BF16)
    return x + _dot(act, wout_ref[...])


def _const_spec(shape):
    nd = len(shape)
    return pl.BlockSpec(shape, lambda *_: (0,) * nd, pipeline_mode=pl.Buffered(1))


def _layer_spec(stacked_shape, layer):
    nd = len(stacked_shape)
    return pl.BlockSpec((None,) + tuple(stacked_shape[1:]),
                        lambda *_: (layer,) + (0,) * (nd - 1), pipeline_mode=pl.Buffered(1))


def _params(n_axes):
    return pltpu.CompilerParams(dimension_semantics=("arbitrary",) * n_axes,
                                vmem_limit_bytes=VMEM_LIMIT_BYTES)


def _ffn_kernel(x_ref, nf_ref, win_ref, wout_ref, o_ref):
    o_ref[...] = _ffn(x_ref[...], nf_ref[...], win_ref, wout_ref)


def _ffn_call(x2d, nf, win, wout, layer, tile):
    n, d = x2d.shape
    return pl.pallas_call(
        _ffn_kernel,
        grid=(n // tile,),
        in_specs=[pl.BlockSpec((tile, d), lambda i: (i, 0)),
                  _layer_spec(nf.shape, layer), _layer_spec(win.shape, layer),
                  _layer_spec(wout.shape, layer)],
        out_specs=pl.BlockSpec((tile, d), lambda i: (i, 0)),
        out_shape=jax.ShapeDtypeStruct((n, d), F32),
        compiler_params=_params(1),
        name="ffn",
    )(x2d, nf, win, wout)


def _ret_head_proj(h, win_ref, hd, dk, dv):
    qk = RET_HEADS * dk
    vd = RET_HEADS * dv
    q = _dot(h, win_ref[:, hd * dk:(hd + 1) * dk])
    k = _dot(h, win_ref[:, qk + hd * dk:qk + (hd + 1) * dk]) * (dk ** -0.5)
    v = _dot(h, win_ref[:, 2 * qk + hd * dv:2 * qk + (hd + 1) * dv])
    gate = _dot(h, win_ref[:, 2 * qk + vd + hd * dv:2 * qk + vd + (hd + 1) * dv])
    return q, k, v, gate


def _ret_prompt_kernel(x_ref, cosr_ref, sinr_ref, cosb_ref, sinb_ref, g_ref, win_ref, wout_ref,
                       xo_ref, s_ref, dmask_ref, qdec_ref, kdec_ref, og_ref):
    t = x_ref.shape[1]
    c = dmask_ref.shape[1]
    dk, dv = s_ref.shape[2], s_ref.shape[3]
    j = pl.program_id(1)

    @pl.when((pl.program_id(0) == 0) & (j == 0))
    def _():
        row = lax.broadcasted_iota(jnp.int32, (c, dk), 0).astype(F32)
        col = lax.broadcasted_iota(jnp.int32, (dk, c), 1).astype(F32)
        for hd in range(RET_HEADS):
            lg = _log_gamma(hd)
            dmask_ref[hd] = _decay_mask(lg, c)
            qdec_ref[hd] = jnp.exp(lg * (row + 1.0))
            kdec_ref[hd] = jnp.exp(lg * (c - 1.0 - col))

    @pl.when(j == 0)
    def _():
        s_ref[...] = jnp.zeros(s_ref.shape, F32)

    x = x_ref[0]
    h = _rms(x, g_ref[...]).astype(BF16)
    cosb, sinb = cosb_ref[...], sinb_ref[...]
    cosr, sinr = cosr_ref[...], sinr_ref[...]
    cos = cosb * cosr - sinb * sinr
    sin = sinb * cosr + cosb * sinr
    qk, vd = RET_HEADS * dk, RET_HEADS * dv
    n_chunks = t // c
    proj = [dict() for _ in range(RET_HEADS)]
    out = [x]

    def proj_steps(hd):
        def q_step():
            proj[hd]["q"] = _rotary(_dot(h, win_ref[:, hd * dk:(hd + 1) * dk]), cos, sin)

        def k_step():
            k = _dot(h, win_ref[:, qk + hd * dk:qk + (hd + 1) * dk]) * (dk ** -0.5)
            proj[hd]["k_t"] = _rotary(k, cos, sin).T

        def v_step():
            lo = 2 * qk + hd * dv
            proj[hd]["v"] = _dot(h, win_ref[:, lo:lo + dv]).astype(BF16)

        def gate_step():
            lo = 2 * qk + vd + hd * dv
            proj[hd]["gate"] = _dot(h, win_ref[:, lo:lo + dv])

        return [q_step, k_step, v_step, gate_step]

    def out_step(hd):
        def step():
            out[0] = out[0] + _dot(og_ref[:, hd * dv:(hd + 1) * dv],
                                   wout_ref[hd * dv:(hd + 1) * dv, :])
        return step

    def core(hd, fillers):
        lg = _log_gamma(hd)
        q, k_t, v, gate = (proj[hd][name] for name in ("q", "k_t", "v", "gate"))
        chunks = [slice(ci * c, (ci + 1) * c) for ci in range(n_chunks)]
        scores = [_dot(q[rows].astype(BF16), k_t[:, rows].astype(BF16)) for rows in chunks]
        fillers.pop(0)()
        s_cur = s_ref[0, hd]
        for rows, sc in zip(chunks, scores):
            vc = v[rows]
            q_dec = (q[rows] * qdec_ref[hd]).astype(BF16)
            k_dec_t = (k_t[:, rows] * kdec_ref[hd]).astype(BF16)
            lhs = jnp.concatenate([(sc * dmask_ref[hd]).astype(BF16), q_dec], axis=1)
            rhs = jnp.concatenate([vc, s_cur.astype(BF16)], axis=0)
            o = _dot(lhs, rhs)
            s_cur = math.exp(lg * c) * s_cur + _dot(k_dec_t, vc)
            if fillers:
                fillers.pop(0)()
            o = o * lax.rsqrt(jnp.mean(o * o, axis=-1, keepdims=True) + EPS)
            og_ref[rows, hd * dv:(hd + 1) * dv] = (_silu(gate[rows]) * o).astype(BF16)
        s_ref[0, hd] = s_cur
        for filler in fillers:
            filler()

    for step in proj_steps(0):
        step()
    for hd in range(RET_HEADS):
        if hd + 1 < RET_HEADS:
            fillers = proj_steps(hd + 1)
        else:
            fillers = [out_step(i) for i in range(hd)]
        core(hd, fillers)
    out_step(RET_HEADS - 1)()
    xo_ref[0] = out[0]


def _ret_prompt_call(x, rope, s_shape, g, win, wout, mix_layer, ret_layer):
    b, l, d = x.shape
    nh, dk, dv = s_shape
    t, c = PROMPT_TILE, RET_CHUNK
    cosr, sinr, cosb, sinb = rope
    half = cosr.shape[1]
    base_spec = pl.BlockSpec((None, 1, half), lambda i, j: (j, 0, 0))
    return pl.pallas_call(
        _ret_prompt_kernel,
        grid=(b, l // t),
        in_specs=[pl.BlockSpec((1, t, d), lambda i, j: (i, j, 0)),
                  _const_spec(cosr.shape), _const_spec(sinr.shape), base_spec, base_spec,
                  _layer_spec(g.shape, mix_layer), _layer_spec(win.shape, ret_layer),
                  _layer_spec(wout.shape, ret_layer)],
        out_specs=[pl.BlockSpec((1, t, d), lambda i, j: (i, j, 0)),
                   pl.BlockSpec((1, nh, dk, dv), lambda i, j: (i, 0, 0, 0))],
        out_shape=[jax.ShapeDtypeStruct((b, l, d), F32),
                   jax.ShapeDtypeStruct((b, nh, dk, dv), F32)],
        scratch_shapes=[pltpu.VMEM((nh, c, c), F32), pltpu.VMEM((nh, c, dk), F32),
                        pltpu.VMEM((nh, dk, c), F32), pltpu.VMEM((t, nh * dv), BF16)],
        compiler_params=_params(2),
        name="ret_prompt",
    )(x, cosr, sinr, cosb, sinb, g, win, wout)


def _ret_sample_kernel(x_ref, cos_ref, sin_ref, s0_ref, g_ref, win_ref, wout_ref, *rest, seg):
    earlier = rest[:-7]
    xo_ref, s_ref, q_ref, k_ref, v_ref, gate_ref, og_ref = rest[-7:]
    if earlier:
        for layer, prev_ref in enumerate(earlier):
            s_ref[layer] = prev_ref[...]
        s_ref = s_ref.at[len(earlier)]
    dk, dv = s_ref.shape[2], s_ref.shape[3]
    s = pl.program_id(0)

    @pl.when(s == 0)
    def _():
        h = _rms(x_ref[...], g_ref[...]).astype(BF16)
        cos, sin = cos_ref[...], sin_ref[...]
        for hd in range(RET_HEADS):
            q, k, v, gate = _ret_head_proj(h, win_ref, hd, dk, dv)
            q_ref[:, hd * dk:(hd + 1) * dk] = _rotary(q, cos, sin)
            k_ref[:, hd * dk:(hd + 1) * dk] = _rotary(k, cos, sin)
            v_ref[:, hd * dv:(hd + 1) * dv] = v
            gate_ref[:, hd * dv:(hd + 1) * dv] = gate

    rows = pl.ds(pl.multiple_of(s * seg, seg), seg)
    idx = lax.broadcasted_iota(jnp.int32, (seg, 1), 0).astype(F32)
    for hd in range(RET_HEADS):
        lg = _log_gamma(hd)
        q = q_ref[rows, hd * dk:(hd + 1) * dk]
        k = k_ref[rows, hd * dk:(hd + 1) * dk]
        v = v_ref[rows, hd * dv:(hd + 1) * dv].astype(BF16)
        gate = gate_ref[rows, hd * dv:(hd + 1) * dv]
        q_dec = (q * jnp.exp(lg * (idx + 1.0))).astype(BF16)
        k_dec = (k * jnp.exp(lg * (seg - 1.0 - idx))).astype(BF16)
        scores = _dot_nt(q.astype(BF16), k.astype(BF16)) * _decay_mask(lg, seg)
        s_old = s0_ref[0, hd]
        o = _dot(scores.astype(BF16), v) + _dot(q_dec, s_old.astype(BF16))
        s_ref[0, hd] = math.exp(lg * seg) * s_old + _dot_tn(k_dec, v)
        o = o * lax.rsqrt(jnp.mean(o * o, axis=-1, keepdims=True) + EPS)
        og_ref[rows, hd * dv:(hd + 1) * dv] = _silu(gate) * o

    @pl.when(s == pl.num_programs(0) - 1)
    def _():
        xo_ref[...] = x_ref[...] + _dot(og_ref[...].astype(BF16), wout_ref[...])


def _ret_sample_call(x, cos, sin, state, g, win, wout, mix_layer, ret_layer, earlier):
    ns, seg, d = x.shape
    _, _, nh, dk, dv = state.shape
    n = ns * seg
    stream_spec = pl.BlockSpec((1, nh, dk, dv), lambda i: (i, 0, 0, 0))
    if earlier:
        n_out = len(earlier) + 1
        out_state = jax.ShapeDtypeStruct((n_out, ns, nh, dk, dv), F32)
        out_state_spec = pl.BlockSpec((n_out, 1, nh, dk, dv), lambda i: (0, i, 0, 0, 0))
    else:
        out_state = jax.ShapeDtypeStruct((ns, nh, dk, dv), F32)
        out_state_spec = stream_spec
    out, s_new = pl.pallas_call(
        functools.partial(_ret_sample_kernel, seg=seg),
        grid=(ns,),
        in_specs=[_const_spec((n, d)), _const_spec(cos.shape), _const_spec(sin.shape),
                  pl.BlockSpec((None, 1, nh, dk, dv), lambda i: (ret_layer, i, 0, 0, 0)),
                  _layer_spec(g.shape, mix_layer), _layer_spec(win.shape, ret_layer),
                  _layer_spec(wout.shape, ret_layer)] + [stream_spec] * len(earlier),
        out_specs=[pl.BlockSpec((n, d), lambda i: (0, 0)), out_state_spec],
        out_shape=[jax.ShapeDtypeStruct((n, d), F32), out_state],
        scratch_shapes=[pltpu.VMEM((n, nh * dk), F32), pltpu.VMEM((n, nh * dk), F32),
                        pltpu.VMEM((n, nh * dv), F32), pltpu.VMEM((n, nh * dv), F32),
                        pltpu.VMEM((n, nh * dv), F32)],
        compiler_params=_params(1),
        name="ret_sample",
    )(x.reshape(n, d), cos, sin, state, g, win, wout, *earlier)
    return out.reshape(ns, seg, d), s_new


def _pool_diff_pieces(x_ref, nm_ref, xp_ref, d_ref, pos_start):
    nseg, lt, d = x_ref.shape
    gw = d // len(POOL_WINDOWS)
    blk = min(lt, POOL_BLOCK)

    def piece(sg, rb, after=None):
        r0 = rb * blk
        gain = nm_ref[...]
        if after is not None:
            known = jnp.concatenate([_known(after)[0:1]] * (d // LANES), axis=1)
            gain = jnp.where(known, gain, 0.0)
        h = _rms(x_ref[sg, r0:r0 + blk, :], gain)
        xp_ref[sg, POOL_HIST + r0:POOL_HIST + r0 + blk, :] = h
        pos = pos_start + r0 + lax.broadcasted_iota(jnp.int32, (blk, 1), 0)
        for gi, w in enumerate(POOL_WINDOWS):
            cols = slice(gi * gw, (gi + 1) * gw)
            acc = xp_ref[sg, r0:r0 + POOL_HIST + blk, cols]
            span = 1
            while span < w:
                acc = acc + pltpu.roll(acc, span, axis=0)
                span *= 2
            cnt = jnp.minimum(pos + 1, w).astype(F32)
            dg = acc[POOL_HIST:, :] / cnt - h[:, cols]
            d_ref[sg * lt + r0:sg * lt + r0 + blk, cols] = dg.astype(BF16)
            token = dg if gi == 0 else token + dg
        token = sum(token[:, i:i + LANES] for i in range(0, gw, LANES))
        return token.reshape(blk // SUBLANES, SUBLANES, LANES).sum(axis=0)

    return [functools.partial(piece, sg, rb) for sg in range(nseg) for rb in range(lt // blk)]


def _known(token):
    bits = pltpu.bitcast(token, jnp.uint32)
    bits = lax.shift_right_logical(lax.shift_right_logical(bits, jnp.uint32(16)), jnp.uint32(16))
    return bits == 0


def _tie(ref, token):
    rows = SUBLANES * 4 // jnp.dtype(ref.dtype).itemsize
    known = jnp.concatenate([_known(token)] * (rows // SUBLANES), axis=0)
    tile = ref[-rows:, -LANES:].astype(F32)
    ref[-rows:, -LANES:] = jnp.where(known, tile, 0.0).astype(ref.dtype)


def _pool_project(d_ref, wp_ref, ps_ref, xp_ref, bo_ref, holder):
    def step(after=None):
        del after
        lt = xp_ref.shape[1] - POOL_HIST
        gw = d_ref.shape[1] // len(POOL_WINDOWS)
        ys = [_dot(d_ref[:, gi * gw:(gi + 1) * gw], wp_ref[gi])
              for gi in range(len(POOL_WINDOWS))]
        holder["y"] = jnp.concatenate(ys, axis=-1) * ps_ref[...]
        tail = xp_ref[:, lt:lt + POOL_HIST, :]
        bo_ref[...] = tail
        xp_ref[:, 0:POOL_HIST, :] = tail
        return sum(y[-SUBLANES:, -LANES:] for y in ys)
    return step


def _ffn_interleaved(x1_ref, nf, win_ref, wout_ref, h_ref, act_ref, fillers):
    d_ff = wout_ref.shape[0]
    n_chunks = d_ff // FFN_CHUNK
    assert len(fillers) < n_chunks
    hf = _rms(x1_ref[...], nf)
    h0 = hf.astype(BF16)
    h_ref[...] = h0
    prev = hf[-SUBLANES:, -LANES:]
    for c in range(n_chunks):
        cols = slice(c * FFN_CHUNK, (c + 1) * FFN_CHUNK)
        gate_cols = slice(d_ff + c * FFN_CHUNK, d_ff + (c + 1) * FFN_CHUNK)
        h = h_ref[...] if c else h0
        up = _dot(h, win_ref[:, gate_cols])
        act_ref[:, cols] = (_silu(_dot(h, win_ref[:, cols])) * up).astype(BF16)
        if c < len(fillers):
            _tie(h_ref, fillers[c](after=prev))
        prev = up[0:SUBLANES, 0:LANES]
    return x1_ref[...] + _dot(act_ref[...], wout_ref[...])


def _pool_ffn_kernel(x_ref, buf_ref, nm_ref, wp_ref, ps_ref, nf_ref, win_ref, wout_ref,
                     nfin_ref, xo_ref, bo_ref, xp_ref, d_ref, h_ref, act_ref, x1_ref, *,
                     pos0, final, seq_tiles, skew):
    nseg, lt, d = x_ref.shape
    s = pl.program_id(0)
    tile = jnp.minimum(s, pl.num_programs(0) - 1 - skew)
    j = tile % seq_tiles

    @pl.when(j == 0)
    def _():
        xp_ref[:, 0:POOL_HIST, :] = buf_ref[...]

    holder = {}
    mixer = _pool_diff_pieces(x_ref, nm_ref, xp_ref, d_ref, pos0 + j * lt)
    mixer.append(_pool_project(d_ref, wp_ref, ps_ref, xp_ref, bo_ref, holder))

    def mixed():
        return x_ref[...].reshape(nseg * lt, d) + holder["y"]

    def ffn_out(fillers):
        x2 = _ffn_interleaved(x1_ref, nf_ref[...], win_ref, wout_ref, h_ref, act_ref, fillers)
        if final:
            x2 = _rms(x2, nfin_ref[...])
        xo_ref[...] = x2.reshape(nseg, lt, d)

    if skew:
        @pl.when(s == 0)
        def _():
            x1_ref[...] = jnp.zeros(x1_ref.shape, F32)

        ffn_out(mixer)
        x1_ref[...] = mixed()
    else:
        for step in mixer:
            step()
        x1_ref[...] = mixed()
        ffn_out([])


def _pool_ffn_call(x, buf16, nm, wp, ps, nf, win, wout, nfin, *, mix_layer, pool_layer,
                   seg_block, tile, pos0, final, skew):
    ns, l, d = x.shape
    seq_tiles = l // tile
    n_tiles = (ns // seg_block) * seq_tiles

    def mixer_tile(s):
        t = jnp.minimum(s, n_tiles - 1)
        return t // seq_tiles, t % seq_tiles

    def ffn_tile(s):
        t = jnp.maximum(s - skew, 0)
        return t // seq_tiles, t % seq_tiles

    rows = seg_block * tile
    return pl.pallas_call(
        functools.partial(_pool_ffn_kernel, pos0=pos0, final=final, seq_tiles=seq_tiles,
                          skew=skew),
        grid=(n_tiles + skew,),
        in_specs=[pl.BlockSpec((seg_block, tile, d), lambda s: (*mixer_tile(s), 0)),
                  pl.BlockSpec((seg_block, POOL_HIST, d), lambda s: (mixer_tile(s)[0], 0, 0)),
                  _layer_spec(nm.shape, mix_layer), _layer_spec(wp.shape, pool_layer),
                  _layer_spec(ps.shape, pool_layer), _layer_spec(nf.shape, mix_layer),
                  _layer_spec(win.shape, mix_layer), _layer_spec(wout.shape, mix_layer),
                  _const_spec(nfin.shape)],
        out_specs=[pl.BlockSpec((seg_block, tile, d), lambda s: (*ffn_tile(s), 0)),
                   pl.BlockSpec((seg_block, POOL_HIST, d), lambda s: (mixer_tile(s)[0], 0, 0))],
        out_shape=[jax.ShapeDtypeStruct((ns, l, d), F32),
                   jax.ShapeDtypeStruct((ns, POOL_HIST, d), F32)],
        scratch_shapes=[pltpu.VMEM((seg_block, POOL_HIST + tile, d), F32),
                        pltpu.VMEM((rows, d), BF16), pltpu.VMEM((rows, d), BF16),
                        pltpu.VMEM((rows, wout.shape[1]), BF16),
                        pltpu.VMEM((rows, d), F32)],
        compiler_params=_params(1),
        name="pool_ffn",
    )(x, buf16, nm, wp, ps, nf, win, wout, nfin)


def _rope_angles(pos, half):
    inv = ROPE_BASE ** (-np.arange(half, dtype=np.float64) / half)
    return np.asarray(pos, np.float64)[:, None] * inv[None, :]


def _cos_sin(ang):
    return jnp.asarray(np.cos(ang), F32), jnp.asarray(np.sin(ang), F32)


def kernel(x_prompt, x_sample, state_ret, state_pool, norm_mix, w_ret_in, w_ret_out, w_pool,
           pool_scale, norm_ffn, w_ffn_in, w_ffn_out, norm_final):
    bp, lp, d = x_prompt.shape
    bs, ls, _ = x_sample.shape
    depth = norm_mix.shape[0]
    _, _, nh, dk, dv = state_ret.shape
    assert nh == RET_HEADS and state_pool.shape[2] == POOL_HIST - 1
    assert lp % PROMPT_TILE == 0 and PROMPT_TILE % RET_CHUNK == 0 and ls >= POOL_HIST
    assert depth % 2 == 0

    w_ret_in_b = w_ret_in.astype(BF16)
    w_ret_out_b = w_ret_out.astype(BF16)
    w_pool_b = w_pool.astype(BF16)
    w_ffn_in_b = w_ffn_in.astype(BF16)
    w_ffn_out_b = w_ffn_out.astype(BF16)
    norm_mix3 = norm_mix[:, None, :]
    norm_ffn3 = norm_ffn[:, None, :]
    pool_scale3 = pool_scale[:, None, :]
    nfin = norm_final[None, :]

    half = dk // 2
    cosr, sinr = _cos_sin(_rope_angles(np.arange(PROMPT_TILE), half))
    cosb, sinb = _cos_sin(_rope_angles(np.arange(0, lp, PROMPT_TILE), half)[:, None, :])
    rope_p = (cosr, sinr, cosb, sinb)
    cos_s, sin_s = _cos_sin(np.tile(_rope_angles(PAST_LEN + np.arange(ls), half), (bs, 1)))

    pool0_p = jnp.zeros((bp, POOL_HIST, d), state_pool.dtype)
    pool0_s = jnp.pad(state_pool, ((0, 0), (0, 0), (1, 0), (0, 0)))

    xp, xs = x_prompt, x_sample
    ret_p, pool_p, pool_s = [], [], []
    n_ret = state_ret.shape[0]
    ret_s = []
    for i in range(depth):
        jm = i // 2
        if i % 2 == 0:
            xp, sp = _ret_prompt_call(xp, rope_p, (nh, dk, dv), norm_mix3, w_ret_in_b,
                                      w_ret_out_b, i, jm)
            earlier = list(ret_s) if jm == n_ret - 1 else []
            xs, ss = _ret_sample_call(xs, cos_s, sin_s, state_ret, norm_mix3, w_ret_in_b,
                                      w_ret_out_b, i, jm, earlier)
            ret_s.append(ss)
            ret_p.append(sp)
            xp = _ffn_call(xp.reshape(bp * lp, d), norm_ffn3, w_ffn_in_b, w_ffn_out_b, i,
                           PROMPT_TILE).reshape(bp, lp, d)
            xs = _ffn_call(xs.reshape(bs * ls, d), norm_ffn3, w_ffn_in_b, w_ffn_out_b, i,
                           bs * ls).reshape(bs, ls, d)
        else:
            kw = dict(mix_layer=i, pool_layer=jm, final=i == depth - 1)
            args = (norm_mix3, w_pool_b, pool_scale3, norm_ffn3, w_ffn_in_b, w_ffn_out_b, nfin)
            xp, bpn = _pool_ffn_call(xp, pool0_p, *args, seg_block=1, tile=PROMPT_TILE, pos0=0,
                                     skew=1, **kw)
            xs, bsn = _pool_ffn_call(xs, pool0_s[jm], *args, seg_block=bs, tile=ls, pos0=PAST_LEN,
                                     skew=0, **kw)
            pool_p.append(bpn[:, 1:, :])
            pool_s.append(bsn[:, 1:, :])
    ret_s_stack = ret_s[-1] if n_ret > 1 else ret_s[0][None]
    return (xp, xs, jnp.stack(ret_p), ret_s_stack, jnp.stack(pool_p), jnp.stack(pool_s))
```

```python
import functools
import math

import numpy as np

import jax
import jax.numpy as jnp
from jax import lax
from jax.experimental import pallas as pl
from jax.experimental.pallas import tpu as pltpu

F32 = jnp.float32
BF16 = jnp.bfloat16

EPS = 1e-6
ROPE_BASE = 10000.0
PAST_LEN = 1024
RET_HEADS = 4
POOL_WINDOWS = (2, 4, 8, 16)
POOL_HIST = 16
PROMPT_TILE = 512
RET_CHUNK = 256
FFN_TILE = 1024
FFN_CHUNK = 256
VMEM_LIMIT_BYTES = 56 * 1024 * 1024


def _log_gamma(head):
    return math.log1p(-(2.0 ** (-5.0 - head)))


def _rms(x, g):
    return (x * lax.rsqrt(jnp.mean(x * x, axis=-1, keepdims=True) + EPS)) * g


def _dot(a, b):
    return jnp.dot(a, b, preferred_element_type=F32)


def _dot_nt(a, b):
    return lax.dot_general(a, b, (((1,), (1,)), ((), ())), preferred_element_type=F32)


def _dot_tn(a, b):
    return lax.dot_general(a, b, (((0,), (0,)), ((), ())), preferred_element_type=F32)


def _rotary(t, cos, sin):
    half = t.shape[-1] // 2
    t1, t2 = t[:, :half], t[:, half:]
    return jnp.concatenate([t1 * cos - t2 * sin, t1 * sin + t2 * cos], axis=-1)


def _silu(t):
    return t * jax.nn.sigmoid(t)


def _decay_mask(lg, n):
    r = lax.broadcasted_iota(jnp.int32, (n, n), 0)
    c = lax.broadcasted_iota(jnp.int32, (n, n), 1)
    diff = (r - c).astype(F32)
    return jnp.where(diff >= 0, jnp.exp(lg * jnp.maximum(diff, 0.0)), 0.0)


def _ffn(x, nf, win_ref, wout_ref, act_ref):
    d_ff = wout_ref.shape[0]
    h = _rms(x, nf).astype(BF16)
    for lo in range(0, d_ff, FFN_CHUNK):
        up = _dot(h, win_ref[:, d_ff + lo:d_ff + lo + FFN_CHUNK])
        act_ref[:, lo:lo + FFN_CHUNK] = (
            _silu(_dot(h, win_ref[:, lo:lo + FFN_CHUNK])) * up).astype(BF16)
    return x + _dot(act_ref[...], wout_ref[...])


def _const_spec(shape):
    nd = len(shape)
    return pl.BlockSpec(shape, lambda *_: (0,) * nd, pipeline_mode=pl.Buffered(1))


def _layer_spec(stacked_shape, layer):
    nd = len(stacked_shape)
    return pl.BlockSpec((None,) + tuple(stacked_shape[1:]),
                        lambda *_: (layer,) + (0,) * (nd - 1), pipeline_mode=pl.Buffered(1))


def _params(n_axes):
    return pltpu.CompilerParams(dimension_semantics=("arbitrary",) * n_axes,
                                vmem_limit_bytes=VMEM_LIMIT_BYTES)


def _ffn_kernel(x_ref, nf_ref, win_ref, wout_ref, o_ref, act_ref):
    o_ref[...] = _ffn(x_ref[...], nf_ref[...], win_ref, wout_ref, act_ref)


def _ffn_call(x2d, nf, win, wout, layer, tile):
    n, d = x2d.shape
    return pl.pallas_call(
        _ffn_kernel,
        grid=(n // tile,),
        in_specs=[pl.BlockSpec((tile, d), lambda i: (i, 0)),
                  _layer_spec(nf.shape, layer), _layer_spec(win.shape, layer),
                  _layer_spec(wout.shape, layer)],
        out_specs=pl.BlockSpec((tile, d), lambda i: (i, 0)),
        out_shape=jax.ShapeDtypeStruct((n, d), F32),
        scratch_shapes=[pltpu.VMEM((tile, wout.shape[1]), BF16)],
        compiler_params=_params(1),
        name="ffn",
    )(x2d, nf, win, wout)


def _ret_head_proj(h, win_ref, hd, dk, dv):
    qk = RET_HEADS * dk
    vd = RET_HEADS * dv
    q = _dot(h, win_ref[:, hd * dk:(hd + 1) * dk])
    k = _dot(h, win_ref[:, qk + hd * dk:qk + (hd + 1) * dk]) * (dk ** -0.5)
    v = _dot(h, win_ref[:, 2 * qk + hd * dv:2 * qk + (hd + 1) * dv])
    gate = _dot(h, win_ref[:, 2 * qk + vd + hd * dv:2 * qk + vd + (hd + 1) * dv])
    return q, k, v, gate


def _ret_prompt_kernel(x_ref, cosr_ref, sinr_ref, cosb_ref, sinb_ref, g_ref, win_ref, wout_ref,
                       xo_ref, s_ref, dmask_ref, qdec_ref, kdec_ref, og_ref):
    t = x_ref.shape[1]
    c = dmask_ref.shape[1]
    dk, dv = s_ref.shape[2], s_ref.shape[3]
    j = pl.program_id(1)

    @pl.when((pl.program_id(0) == 0) & (j == 0))
    def _():
        row = lax.broadcasted_iota(jnp.int32, (c, dk), 0).astype(F32)
        col = lax.broadcasted_iota(jnp.int32, (dk, c), 1).astype(F32)
        for hd in range(RET_HEADS):
            lg = _log_gamma(hd)
            dmask_ref[hd] = _decay_mask(lg, c)
            qdec_ref[hd] = jnp.exp(lg * (row + 1.0))
            kdec_ref[hd] = jnp.exp(lg * (c - 1.0 - col))

    @pl.when(j == 0)
    def _():
        s_ref[...] = jnp.zeros(s_ref.shape, F32)

    x = x_ref[0]
    h = _rms(x, g_ref[...]).astype(BF16)
    cosb, sinb = cosb_ref[...], sinb_ref[...]
    cosr, sinr = cosr_ref[...], sinr_ref[...]
    cos = cosb * cosr - sinb * sinr
    sin = sinb * cosr + cosb * sinr
    qk, vd = RET_HEADS * dk, RET_HEADS * dv
    n_chunks = t // c
    proj = [dict() for _ in range(RET_HEADS)]
    out = [x]

    def proj_steps(hd):
        def q_step():
            proj[hd]["q"] = _rotary(_dot(h, win_ref[:, hd * dk:(hd + 1) * dk]), cos, sin)

        def k_step():
            k = _dot(h, win_ref[:, qk + hd * dk:qk + (hd + 1) * dk]) * (dk ** -0.5)
            proj[hd]["k_t"] = _rotary(k, cos, sin).T

        def v_step():
            lo = 2 * qk + hd * dv
            proj[hd]["v"] = _dot(h, win_ref[:, lo:lo + dv]).astype(BF16)

        def gate_step():
            lo = 2 * qk + vd + hd * dv
            proj[hd]["gate"] = _dot(h, win_ref[:, lo:lo + dv])

        return [q_step, k_step, v_step, gate_step]

    def out_step(hd):
        def step():
            out[0] = out[0] + _dot(og_ref[:, hd * dv:(hd + 1) * dv],
                                   wout_ref[hd * dv:(hd + 1) * dv, :])
        return step

    def core(hd, fillers):
        lg = _log_gamma(hd)
        q, k_t, v, gate = (proj[hd][name] for name in ("q", "k_t", "v", "gate"))
        chunks = [slice(ci * c, (ci + 1) * c) for ci in range(n_chunks)]
        scores = [_dot(q[rows].astype(BF16), k_t[:, rows].astype(BF16)) for rows in chunks]
        fillers.pop(0)()
        s_cur = s_ref[0, hd]
        for rows, sc in zip(chunks, scores):
            vc = v[rows]
            q_dec = (q[rows] * qdec_ref[hd]).astype(BF16)
            k_dec_t = (k_t[:, rows] * kdec_ref[hd]).astype(BF16)
            lhs = jnp.concatenate([(sc * dmask_ref[hd]).astype(BF16), q_dec], axis=1)
            rhs = jnp.concatenate([vc, s_cur.astype(BF16)], axis=0)
            o = _dot(lhs, rhs)
            s_cur = math.exp(lg * c) * s_cur + _dot(k_dec_t, vc)
            if fillers:
                fillers.pop(0)()
            o = o * lax.rsqrt(jnp.mean(o * o, axis=-1, keepdims=True) + EPS)
            og_ref[rows, hd * dv:(hd + 1) * dv] = (_silu(gate[rows]) * o).astype(BF16)
        s_ref[0, hd] = s_cur
        for filler in fillers:
            filler()

    for step in proj_steps(0):
        step()
    for hd in range(RET_HEADS):
        if hd + 1 < RET_HEADS:
            fillers = proj_steps(hd + 1)
        else:
            fillers = [out_step(i) for i in range(hd)]
        core(hd, fillers)
    out_step(RET_HEADS - 1)()
    xo_ref[0] = out[0]


def _ret_prompt_call(x, rope, s_shape, g, win, wout, mix_layer, ret_layer):
    b, l, d = x.shape
    nh, dk, dv = s_shape
    t, c = PROMPT_TILE, RET_CHUNK
    cosr, sinr, cosb, sinb = rope
    half = cosr.shape[1]
    base_spec = pl.BlockSpec((None, 1, half), lambda i, j: (j, 0, 0))
    return pl.pallas_call(
        _ret_prompt_kernel,
        grid=(b, l // t),
        in_specs=[pl.BlockSpec((1, t, d), lambda i, j: (i, j, 0)),
                  _const_spec(cosr.shape), _const_spec(sinr.shape), base_spec, base_spec,
                  _layer_spec(g.shape, mix_layer), _layer_spec(win.shape, ret_layer),
                  _layer_spec(wout.shape, ret_layer)],
        out_specs=[pl.BlockSpec((1, t, d), lambda i, j: (i, j, 0)),
                   pl.BlockSpec((1, nh, dk, dv), lambda i, j: (i, 0, 0, 0))],
        out_shape=[jax.ShapeDtypeStruct((b, l, d), F32),
                   jax.ShapeDtypeStruct((b, nh, dk, dv), F32)],
        scratch_shapes=[pltpu.VMEM((nh, c, c), F32), pltpu.VMEM((nh, c, dk), F32),
                        pltpu.VMEM((nh, dk, c), F32), pltpu.VMEM((t, nh * dv), BF16)],
        compiler_params=_params(2),
        name="ret_prompt",
    )(x, cosr, sinr, cosb, sinb, g, win, wout)


def _ret_sample_kernel(x_ref, cos_ref, sin_ref, s0_ref, g_ref, win_ref, wout_ref, *rest, seg):
    earlier = rest[:-7]
    xo_ref, s_ref, q_ref, k_ref, v_ref, gate_ref, og_ref = rest[-7:]
    if earlier:
        for layer, prev_ref in enumerate(earlier):
            s_ref[layer] = prev_ref[...]
        s_ref = s_ref.at[len(earlier)]
    dk, dv = s_ref.shape[2], s_ref.shape[3]
    s = pl.program_id(0)

    @pl.when(s == 0)
    def _():
        h = _rms(x_ref[...], g_ref[...]).astype(BF16)
        cos, sin = cos_ref[...], sin_ref[...]
        for hd in range(RET_HEADS):
            q, k, v, gate = _ret_head_proj(h, win_ref, hd, dk, dv)
            q_ref[:, hd * dk:(hd + 1) * dk] = _rotary(q, cos, sin)
            k_ref[:, hd * dk:(hd + 1) * dk] = _rotary(k, cos, sin)
            v_ref[:, hd * dv:(hd + 1) * dv] = v
            gate_ref[:, hd * dv:(hd + 1) * dv] = gate

    rows = pl.ds(pl.multiple_of(s * seg, seg), seg)
    idx = lax.broadcasted_iota(jnp.int32, (seg, 1), 0).astype(F32)
    for hd in range(RET_HEADS):
        lg = _log_gamma(hd)
        q = q_ref[rows, hd * dk:(hd + 1) * dk]
        k = k_ref[rows, hd * dk:(hd + 1) * dk]
        v = v_ref[rows, hd * dv:(hd + 1) * dv].astype(BF16)
        gate = gate_ref[rows, hd * dv:(hd + 1) * dv]
        q_dec = (q * jnp.exp(lg * (idx + 1.0))).astype(BF16)
        k_dec = (k * jnp.exp(lg * (seg - 1.0 - idx))).astype(BF16)
        scores = _dot_nt(q.astype(BF16), k.astype(BF16)) * _decay_mask(lg, seg)
        s_old = s0_ref[0, hd]
        o = _dot(scores.astype(BF16), v) + _dot(q_dec, s_old.astype(BF16))
        s_ref[0, hd] = math.exp(lg * seg) * s_old + _dot_tn(k_dec, v)
        o = o * lax.rsqrt(jnp.mean(o * o, axis=-1, keepdims=True) + EPS)
        og_ref[rows, hd * dv:(hd + 1) * dv] = _silu(gate) * o

    @pl.when(s == pl.num_programs(0) - 1)
    def _():
        xo_ref[...] = x_ref[...] + _dot(og_ref[...].astype(BF16), wout_ref[...])


def _ret_sample_call(x, cos, sin, state, g, win, wout, mix_layer, ret_layer, earlier):
    ns, seg, d = x.shape
    _, _, nh, dk, dv = state.shape
    n = ns * seg
    stream_spec = pl.BlockSpec((1, nh, dk, dv), lambda i: (i, 0, 0, 0))
    if earlier:
        n_out = len(earlier) + 1
        out_state = jax.ShapeDtypeStruct((n_out, ns, nh, dk, dv), F32)
        out_state_spec = pl.BlockSpec((n_out, 1, nh, dk, dv), lambda i: (0, i, 0, 0, 0))
    else:
        out_state = jax.ShapeDtypeStruct((ns, nh, dk, dv), F32)
        out_state_spec = stream_spec
    out, s_new = pl.pallas_call(
        functools.partial(_ret_sample_kernel, seg=seg),
        grid=(ns,),
        in_specs=[_const_spec((n, d)), _const_spec(cos.shape), _const_spec(sin.shape),
                  pl.BlockSpec((None, 1, nh, dk, dv), lambda i: (ret_layer, i, 0, 0, 0)),
                  _layer_spec(g.shape, mix_layer), _layer_spec(win.shape, ret_layer),
                  _layer_spec(wout.shape, ret_layer)] + [stream_spec] * len(earlier),
        out_specs=[pl.BlockSpec((n, d), lambda i: (0, 0)), out_state_spec],
        out_shape=[jax.ShapeDtypeStruct((n, d), F32), out_state],
        scratch_shapes=[pltpu.VMEM((n, nh * dk), F32), pltpu.VMEM((n, nh * dk), F32),
                        pltpu.VMEM((n, nh * dv), F32), pltpu.VMEM((n, nh * dv), F32),
                        pltpu.VMEM((n, nh * dv), F32)],
        compiler_params=_params(1),
        name="ret_sample",
    )(x.reshape(n, d), cos, sin, state, g, win, wout, *earlier)
    return out.reshape(ns, seg, d), s_new


def _pool_mixer(x_ref, nm_ref, wp_ref, ps_ref, xp_ref, bo_ref, pos_start):
    nseg, lt, d = x_ref.shape
    gw = d // len(POOL_WINDOWS)
    x = x_ref[...]
    h = _rms(x, nm_ref[...])
    xp_ref[:, POOL_HIST:, :] = h
    pos = pos_start + lax.broadcasted_iota(jnp.int32, (1, lt, 1), 1)
    ys = []
    for gi, w in enumerate(POOL_WINDOWS):
        cols = slice(gi * gw, (gi + 1) * gw)
        acc = xp_ref[:, :, cols]
        span = 1
        while span < w:
            acc = acc + pltpu.roll(acc, span, axis=1)
            span *= 2
        cnt = jnp.minimum(pos + 1, w).astype(F32)
        dg = (acc[:, POOL_HIST:, :] / cnt - h[:, :, cols]).astype(BF16).reshape(nseg * lt, gw)
        ys.append(_dot(dg, wp_ref[gi]))
    y = jnp.concatenate(ys, axis=-1) * ps_ref[...]
    tail = xp_ref[:, lt:lt + POOL_HIST, :]
    bo_ref[...] = tail
    xp_ref[:, 0:POOL_HIST, :] = tail
    return x.reshape(nseg * lt, d) + y


def _pool_ffn_kernel(x_ref, buf_ref, nm_ref, wp_ref, ps_ref, nf_ref, win_ref, wout_ref,
                     nfin_ref, xo_ref, bo_ref, xp_ref, act_ref, *, pos0, final):
    nseg, lt, d = x_ref.shape
    j = pl.program_id(1)

    @pl.when(j == 0)
    def _():
        xp_ref[:, 0:POOL_HIST, :] = buf_ref[...]

    x1 = _pool_mixer(x_ref, nm_ref, wp_ref, ps_ref, xp_ref, bo_ref, pos0 + j * lt)
    x2 = _ffn(x1, nf_ref[...], win_ref, wout_ref, act_ref)
    if final:
        x2 = _rms(x2, nfin_ref[...])
    xo_ref[...] = x2.reshape(nseg, lt, d)


def _pool_ffn_call(x, buf16, nm, wp, ps, nf, win, wout, nfin, *, mix_layer, pool_layer,
                   seg_block, tile, pos0, final):
    ns, l, d = x.shape
    return pl.pallas_call(
        functools.partial(_pool_ffn_kernel, pos0=pos0, final=final),
        grid=(ns // seg_block, l // tile),
        in_specs=[pl.BlockSpec((seg_block, tile, d), lambda i, j: (i, j, 0)),
                  pl.BlockSpec((seg_block, POOL_HIST, d), lambda i, j: (i, 0, 0)),
                  _layer_spec(nm.shape, mix_layer), _layer_spec(wp.shape, pool_layer),
                  _layer_spec(ps.shape, pool_layer), _layer_spec(nf.shape, mix_layer),
                  _layer_spec(win.shape, mix_layer), _layer_spec(wout.shape, mix_layer),
                  _const_spec(nfin.shape)],
        out_specs=[pl.BlockSpec((seg_block, tile, d), lambda i, j: (i, j, 0)),
                   pl.BlockSpec((seg_block, POOL_HIST, d), lambda i, j: (i, 0, 0))],
        out_shape=[jax.ShapeDtypeStruct((ns, l, d), F32),
                   jax.ShapeDtypeStruct((ns, POOL_HIST, d), F32)],
        scratch_shapes=[pltpu.VMEM((seg_block, POOL_HIST + tile, d), F32),
                        pltpu.VMEM((seg_block * tile, wout.shape[1]), BF16)],
        compiler_params=_params(2),
        name="pool_ffn",
    )(x, buf16, nm, wp, ps, nf, win, wout, nfin)


def _rope_angles(pos, half):
    inv = ROPE_BASE ** (-np.arange(half, dtype=np.float64) / half)
    return np.asarray(pos, np.float64)[:, None] * inv[None, :]


def _cos_sin(ang):
    return jnp.asarray(np.cos(ang), F32), jnp.asarray(np.sin(ang), F32)


def kernel(x_prompt, x_sample, state_ret, state_pool, norm_mix, w_ret_in, w_ret_out, w_pool,
           pool_scale, norm_ffn, w_ffn_in, w_ffn_out, norm_final):
    bp, lp, d = x_prompt.shape
    bs, ls, _ = x_sample.shape
    depth = norm_mix.shape[0]
    n_ret, _, nh, dk, dv = state_ret.shape
    assert nh == RET_HEADS and state_pool.shape[2] == POOL_HIST - 1
    assert lp % PROMPT_TILE == 0 and PROMPT_TILE % RET_CHUNK == 0 and ls >= POOL_HIST
    assert (bp * lp) % FFN_TILE == 0
    assert depth % 2 == 0

    w_ret_in_b = w_ret_in.astype(BF16)
    w_ret_out_b = w_ret_out.astype(BF16)
    w_pool_b = w_pool.astype(BF16)
    w_ffn_in_b = w_ffn_in.astype(BF16)
    w_ffn_out_b = w_ffn_out.astype(BF16)
    norm_mix3 = norm_mix[:, None, :]
    norm_ffn3 = norm_ffn[:, None, :]
    pool_scale3 = pool_scale[:, None, :]
    nfin = norm_final[None, :]

    half = dk // 2
    cosr, sinr = _cos_sin(_rope_angles(np.arange(PROMPT_TILE), half))
    cosb, sinb = _cos_sin(_rope_angles(np.arange(0, lp, PROMPT_TILE), half)[:, None, :])
    rope_p = (cosr, sinr, cosb, sinb)
    cos_s, sin_s = _cos_sin(np.tile(_rope_angles(PAST_LEN + np.arange(ls), half), (bs, 1)))

    pool0_p = jnp.zeros((bp, POOL_HIST, d), state_pool.dtype)
    pool0_s = jnp.pad(state_pool, ((0, 0), (0, 0), (1, 0), (0, 0)))

    xp, xs = x_prompt, x_sample
    ret_p, ret_s, pool_p, pool_s = [], [], [], []
    for i in range(depth):
        jm = i // 2
        if i % 2 == 0:
            xp, sp = _ret_prompt_call(xp, rope_p, (nh, dk, dv), norm_mix3, w_ret_in_b,
                                      w_ret_out_b, i, jm)
            earlier = list(ret_s) if jm == n_ret - 1 else []
            xs, ss = _ret_sample_call(xs, cos_s, sin_s, state_ret, norm_mix3, w_ret_in_b,
                                      w_ret_out_b, i, jm, earlier)
            ret_s.append(ss)
            ret_p.append(sp)
            xp = _ffn_call(xp.reshape(bp * lp, d), norm_ffn3, w_ffn_in_b, w_ffn_out_b, i,
                           FFN_TILE).reshape(bp, lp, d)
            xs = _ffn_call(xs.reshape(bs * ls, d), norm_ffn3, w_ffn_in_b, w_ffn_out_b, i,
                           bs * ls).reshape(bs, ls, d)
        else:
            kw = dict(mix_layer=i, pool_layer=jm, final=i == depth - 1)
            args = (norm_mix3, w_pool_b, pool_scale3, norm_ffn3, w_ffn_in_b, w_ffn_out_b, nfin)
            xp, bpn = _pool_ffn_call(xp, pool0_p, *args, seg_block=1, tile=PROMPT_TILE, pos0=0, **kw)
            xs, bsn = _pool_ffn_call(xs, pool0_s[jm], *args, seg_block=bs, tile=ls, pos0=PAST_LEN,
                                     **kw)
            pool_p.append(bpn[:, 1:, :])
            pool_s.append(bsn[:, 1:, :])
    ret_s_stack = ret_s[-1] if n_ret > 1 else ret_s[0][None]
    return (xp, xs, jnp.stack(ret_p), ret_s_stack, jnp.stack(pool_p), jnp.stack(pool_s))
```

```python
import functools
import math

import numpy as np

import jax
import jax.numpy as jnp
from jax import lax
from jax.experimental import pallas as pl
from jax.experimental.pallas import tpu as pltpu

F32 = jnp.float32
BF16 = jnp.bfloat16

EPS = 1e-6
ROPE_BASE = 10000.0
PAST_LEN = 1024
RET_HEADS = 4
POOL_WINDOWS = (2, 4, 8, 16)
POOL_HIST = 16
PROMPT_TILE = 512
RET_CHUNK = 256
FFN_TILE = 1024
FFN_CHUNK = 256
BF16_ROWS = 16
VMEM_LIMIT_BYTES = 56 * 1024 * 1024


def _log_gamma(head):
    return math.log1p(-(2.0 ** (-5.0 - head)))


def _rms(x, g):
    return (x * lax.rsqrt(jnp.mean(x * x, axis=-1, keepdims=True) + EPS)) * g


def _dot(a, b):
    return jnp.dot(a, b, preferred_element_type=F32)


def _dot_nt(a, b):
    return lax.dot_general(a, b, (((1,), (1,)), ((), ())), preferred_element_type=F32)


def _dot_tn(a, b):
    return lax.dot_general(a, b, (((0,), (0,)), ((), ())), preferred_element_type=F32)


def _rotary(t, cos, sin):
    half = t.shape[-1] // 2
    t1, t2 = t[:, :half], t[:, half:]
    return jnp.concatenate([t1 * cos - t2 * sin, t1 * sin + t2 * cos], axis=-1)


def _silu(t):
    return t * jax.nn.sigmoid(t)


def _decay_mask(lg, n):
    r = lax.broadcasted_iota(jnp.int32, (n, n), 0)
    c = lax.broadcasted_iota(jnp.int32, (n, n), 1)
    diff = (r - c).astype(F32)
    return jnp.where(diff >= 0, jnp.exp(lg * jnp.maximum(diff, 0.0)), 0.0)


def _ffn(x, nf, win_ref, wout_ref, act_ref):
    d_ff = wout_ref.shape[0]
    h = _rms(x, nf).astype(BF16)
    for lo in range(0, d_ff, FFN_CHUNK):
        up = _dot(h, win_ref[:, d_ff + lo:d_ff + lo + FFN_CHUNK])
        act_ref[:, lo:lo + FFN_CHUNK] = (
            _silu(_dot(h, win_ref[:, lo:lo + FFN_CHUNK])) * up).astype(BF16)
    return x + _dot(act_ref[...], wout_ref[...])


def _const_spec(shape):
    nd = len(shape)
    return pl.BlockSpec(shape, lambda *_: (0,) * nd, pipeline_mode=pl.Buffered(1))


def _layer_spec(stacked_shape, layer):
    nd = len(stacked_shape)
    return pl.BlockSpec((None,) + tuple(stacked_shape[1:]),
                        lambda *_: (layer,) + (0,) * (nd - 1), pipeline_mode=pl.Buffered(1))


def _params(n_axes):
    return pltpu.CompilerParams(dimension_semantics=("arbitrary",) * n_axes,
                                vmem_limit_bytes=VMEM_LIMIT_BYTES)


class _Casts:
    def __init__(self, weights, grid):
        self.grid = grid
        self.in_specs, self.out_specs, self.out_shape, self.operands, self.n_blocks = [], [], [], [], []
        n_steps = math.prod(grid)
        for w, layer in weights:
            _, rows, cols = w.shape
            blk = next(b for b in range(BF16_ROWS, rows + 1, BF16_ROWS)
                       if rows % b == 0 and rows // b <= n_steps)
            n_blk = rows // blk
            self.in_specs.append(pl.BlockSpec(
                (None, blk, cols),
                lambda *g, n_blk=n_blk, layer=layer: (layer, jnp.minimum(self.step(*g), n_blk - 1), 0)))
            self.out_specs.append(pl.BlockSpec(
                (blk, cols), lambda *g, n_blk=n_blk: (jnp.minimum(self.step(*g), n_blk - 1), 0)))
            self.out_shape.append(jax.ShapeDtypeStruct((rows, cols), BF16))
            self.operands.append(w)
            self.n_blocks.append(n_blk)

    def __len__(self):
        return len(self.operands)

    def step(self, *g):
        flat = g[0]
        for idx, extent in zip(g[1:], self.grid[1:]):
            flat = flat * extent + idx
        return flat

    def wrap(self, body, n_in, n_out):
        n = len(self)

        def kernel(*refs):
            ins, srcs = refs[:n_in], refs[n_in:n_in + n]
            outs = refs[n_in + n:n_in + n + n_out]
            dsts = refs[n_in + n + n_out:n_in + 2 * n + n_out]
            body(*ins, *outs, *refs[n_in + 2 * n + n_out:])
            step = self.step(*(pl.program_id(a) for a in range(len(self.grid))))
            for src, dst, n_blk in zip(srcs, dsts, self.n_blocks):
                @pl.when(step < n_blk)
                def _():
                    dst[...] = src[...].astype(BF16)

        return kernel


def _ffn_kernel(x_ref, nf_ref, win_ref, wout_ref, o_ref, act_ref):
    o_ref[...] = _ffn(x_ref[...], nf_ref[...], win_ref, wout_ref, act_ref)


def _ffn_call(x2d, nf, win, wout, layer, tile, cast_weights=()):
    n, d = x2d.shape
    grid = (n // tile,)
    casts = _Casts(cast_weights, grid)
    return pl.pallas_call(
        casts.wrap(_ffn_kernel, 4, 1),
        grid=grid,
        in_specs=[pl.BlockSpec((tile, d), lambda i: (i, 0)), _layer_spec(nf.shape, layer),
                  _const_spec(win.shape), _const_spec(wout.shape)] + casts.in_specs,
        out_specs=[pl.BlockSpec((tile, d), lambda i: (i, 0))] + casts.out_specs,
        out_shape=[jax.ShapeDtypeStruct((n, d), F32)] + casts.out_shape,
        scratch_shapes=[pltpu.VMEM((tile, wout.shape[0]), BF16)],
        compiler_params=_params(1),
        name="ffn",
    )(x2d, nf, win, wout, *casts.operands)


def _ret_head_proj(h, win_ref, hd, dk, dv):
    qk = RET_HEADS * dk
    vd = RET_HEADS * dv
    q = _dot(h, win_ref[:, hd * dk:(hd + 1) * dk])
    k = _dot(h, win_ref[:, qk + hd * dk:qk + (hd + 1) * dk]) * (dk ** -0.5)
    v = _dot(h, win_ref[:, 2 * qk + hd * dv:2 * qk + (hd + 1) * dv])
    gate = _dot(h, win_ref[:, 2 * qk + vd + hd * dv:2 * qk + vd + (hd + 1) * dv])
    return q, k, v, gate


def _ret_prompt_kernel(x_ref, cosr_ref, sinr_ref, cosb_ref, sinb_ref, g_ref, win_ref, wout_ref,
                       xo_ref, s_ref, dmask_ref, qdec_ref, kdec_ref, og_ref):
    t = x_ref.shape[1]
    c = dmask_ref.shape[1]
    dk, dv = s_ref.shape[2], s_ref.shape[3]
    j = pl.program_id(1)

    @pl.when((pl.program_id(0) == 0) & (j == 0))
    def _():
        row = lax.broadcasted_iota(jnp.int32, (c, dk), 0).astype(F32)
        col = lax.broadcasted_iota(jnp.int32, (dk, c), 1).astype(F32)
        for hd in range(RET_HEADS):
            lg = _log_gamma(hd)
            dmask_ref[hd] = _decay_mask(lg, c)
            qdec_ref[hd] = jnp.exp(lg * (row + 1.0))
            kdec_ref[hd] = jnp.exp(lg * (c - 1.0 - col))

    @pl.when(j == 0)
    def _():
        s_ref[...] = jnp.zeros(s_ref.shape, F32)

    x = x_ref[0]
    h = _rms(x, g_ref[...]).astype(BF16)
    cosb, sinb = cosb_ref[...], sinb_ref[...]
    cosr, sinr = cosr_ref[...], sinr_ref[...]
    cos = cosb * cosr - sinb * sinr
    sin = sinb * cosr + cosb * sinr
    qk, vd = RET_HEADS * dk, RET_HEADS * dv
    n_chunks = t // c
    proj = [dict() for _ in range(RET_HEADS)]
    out = [x]

    def proj_steps(hd):
        def q_step():
            proj[hd]["q"] = _rotary(_dot(h, win_ref[:, hd * dk:(hd + 1) * dk]), cos, sin)

        def k_step():
            k = _dot(h, win_ref[:, qk + hd * dk:qk + (hd + 1) * dk]) * (dk ** -0.5)
            proj[hd]["k_t"] = _rotary(k, cos, sin).T

        def v_step():
            lo = 2 * qk + hd * dv
            proj[hd]["v"] = _dot(h, win_ref[:, lo:lo + dv]).astype(BF16)

        def gate_step():
            lo = 2 * qk + vd + hd * dv
            proj[hd]["gate"] = _dot(h, win_ref[:, lo:lo + dv])

        return [q_step, k_step, v_step, gate_step]

    def out_step(hd):
        def step():
            out[0] = out[0] + _dot(og_ref[:, hd * dv:(hd + 1) * dv],
                                   wout_ref[hd * dv:(hd + 1) * dv, :])
        return step

    def core(hd, fillers):
        lg = _log_gamma(hd)
        q, k_t, v, gate = (proj[hd][name] for name in ("q", "k_t", "v", "gate"))
        chunks = [slice(ci * c, (ci + 1) * c) for ci in range(n_chunks)]
        scores = [_dot(q[rows].astype(BF16), k_t[:, rows].astype(BF16)) for rows in chunks]
        fillers.pop(0)()
        s_cur = s_ref[0, hd]
        for rows, sc in zip(chunks, scores):
            vc = v[rows]
            q_dec = (q[rows] * qdec_ref[hd]).astype(BF16)
            k_dec_t = (k_t[:, rows] * kdec_ref[hd]).astype(BF16)
            lhs = jnp.concatenate([(sc * dmask_ref[hd]).astype(BF16), q_dec], axis=1)
            rhs = jnp.concatenate([vc, s_cur.astype(BF16)], axis=0)
            o = _dot(lhs, rhs)
            s_cur = math.exp(lg * c) * s_cur + _dot(k_dec_t, vc)
            if fillers:
                fillers.pop(0)()
            o = o * lax.rsqrt(jnp.mean(o * o, axis=-1, keepdims=True) + EPS)
            og_ref[rows, hd * dv:(hd + 1) * dv] = (_silu(gate[rows]) * o).astype(BF16)
        s_ref[0, hd] = s_cur
        for filler in fillers:
            filler()

    for step in proj_steps(0):
        step()
    for hd in range(RET_HEADS):
        if hd + 1 < RET_HEADS:
            fillers = proj_steps(hd + 1)
        else:
            fillers = [out_step(i) for i in range(hd)]
        core(hd, fillers)
    out_step(RET_HEADS - 1)()
    xo_ref[0] = out[0]


def _ret_prompt_call(x, rope, s_shape, g, win, wout, mix_layer, cast_weights=()):
    b, l, d = x.shape
    nh, dk, dv = s_shape
    t, c = PROMPT_TILE, RET_CHUNK
    cosr, sinr, cosb, sinb = rope
    half = cosr.shape[1]
    base_spec = pl.BlockSpec((None, 1, half), lambda i, j: (j, 0, 0))
    grid = (b, l // t)
    casts = _Casts(cast_weights, grid)
    return pl.pallas_call(
        casts.wrap(_ret_prompt_kernel, 8, 2),
        grid=grid,
        in_specs=[pl.BlockSpec((1, t, d), lambda i, j: (i, j, 0)),
                  _const_spec(cosr.shape), _const_spec(sinr.shape), base_spec, base_spec,
                  _layer_spec(g.shape, mix_layer), _const_spec(win.shape),
                  _const_spec(wout.shape)] + casts.in_specs,
        out_specs=[pl.BlockSpec((1, t, d), lambda i, j: (i, j, 0)),
                   pl.BlockSpec((1, nh, dk, dv), lambda i, j: (i, 0, 0, 0))] + casts.out_specs,
        out_shape=[jax.ShapeDtypeStruct((b, l, d), F32),
                   jax.ShapeDtypeStruct((b, nh, dk, dv), F32)] + casts.out_shape,
        scratch_shapes=[pltpu.VMEM((nh, c, c), F32), pltpu.VMEM((nh, c, dk), F32),
                        pltpu.VMEM((nh, dk, c), F32), pltpu.VMEM((t, nh * dv), BF16)],
        compiler_params=_params(2),
        name="ret_prompt",
    )(x, cosr, sinr, cosb, sinb, g, win, wout, *casts.operands)


def _ret_sample_kernel(x_ref, cos_ref, sin_ref, s0_ref, g_ref, win_ref, wout_ref, *rest, seg):
    earlier = rest[:-7]
    xo_ref, s_ref, q_ref, k_ref, v_ref, gate_ref, og_ref = rest[-7:]
    if earlier:
        for layer, prev_ref in enumerate(earlier):
            s_ref[layer] = prev_ref[...]
        s_ref = s_ref.at[len(earlier)]
    dk, dv = s_ref.shape[2], s_ref.shape[3]
    s = pl.program_id(0)

    @pl.when(s == 0)
    def _():
        h = _rms(x_ref[...], g_ref[...]).astype(BF16)
        cos, sin = cos_ref[...], sin_ref[...]
        for hd in range(RET_HEADS):
            q, k, v, gate = _ret_head_proj(h, win_ref, hd, dk, dv)
            q_ref[:, hd * dk:(hd + 1) * dk] = _rotary(q, cos, sin)
            k_ref[:, hd * dk:(hd + 1) * dk] = _rotary(k, cos, sin)
            v_ref[:, hd * dv:(hd + 1) * dv] = v
            gate_ref[:, hd * dv:(hd + 1) * dv] = gate

    rows = pl.ds(pl.multiple_of(s * seg, seg), seg)
    idx = lax.broadcasted_iota(jnp.int32, (seg, 1), 0).astype(F32)
    for hd in range(RET_HEADS):
        lg = _log_gamma(hd)
        q = q_ref[rows, hd * dk:(hd + 1) * dk]
        k = k_ref[rows, hd * dk:(hd + 1) * dk]
        v = v_ref[rows, hd * dv:(hd + 1) * dv].astype(BF16)
        gate = gate_ref[rows, hd * dv:(hd + 1) * dv]
        q_dec = (q * jnp.exp(lg * (idx + 1.0))).astype(BF16)
        k_dec = (k * jnp.exp(lg * (seg - 1.0 - idx))).astype(BF16)
        scores = _dot_nt(q.astype(BF16), k.astype(BF16)) * _decay_mask(lg, seg)
        s_old = s0_ref[0, hd]
        o = _dot(scores.astype(BF16), v) + _dot(q_dec, s_old.astype(BF16))
        s_ref[0, hd] = math.exp(lg * seg) * s_old + _dot_tn(k_dec, v)
        o = o * lax.rsqrt(jnp.mean(o * o, axis=-1, keepdims=True) + EPS)
        og_ref[rows, hd * dv:(hd + 1) * dv] = _silu(gate) * o

    @pl.when(s == pl.num_programs(0) - 1)
    def _():
        xo_ref[...] = x_ref[...] + _dot(og_ref[...].astype(BF16), wout_ref[...])


def _ret_sample_call(x, cos, sin, state, g, win, wout, mix_layer, ret_layer, earlier):
    ns, seg, d = x.shape
    _, _, nh, dk, dv = state.shape
    n = ns * seg
    stream_spec = pl.BlockSpec((1, nh, dk, dv), lambda i: (i, 0, 0, 0))
    if earlier:
        n_out = len(earlier) + 1
        out_state = jax.ShapeDtypeStruct((n_out, ns, nh, dk, dv), F32)
        out_state_spec = pl.BlockSpec((n_out, 1, nh, dk, dv), lambda i: (0, i, 0, 0, 0))
    else:
        out_state = jax.ShapeDtypeStruct((ns, nh, dk, dv), F32)
        out_state_spec = stream_spec
    out, s_new = pl.pallas_call(
        functools.partial(_ret_sample_kernel, seg=seg),
        grid=(ns,),
        in_specs=[_const_spec((n, d)), _const_spec(cos.shape), _const_spec(sin.shape),
                  pl.BlockSpec((None, 1, nh, dk, dv), lambda i: (ret_layer, i, 0, 0, 0)),
                  _layer_spec(g.shape, mix_layer), _const_spec(win.shape),
                  _const_spec(wout.shape)] + [stream_spec] * len(earlier),
        out_specs=[pl.BlockSpec((n, d), lambda i: (0, 0)), out_state_spec],
        out_shape=[jax.ShapeDtypeStruct((n, d), F32), out_state],
        scratch_shapes=[pltpu.VMEM((n, nh * dk), F32), pltpu.VMEM((n, nh * dk), F32),
                        pltpu.VMEM((n, nh * dv), F32), pltpu.VMEM((n, nh * dv), F32),
                        pltpu.VMEM((n, nh * dv), F32)],
        compiler_params=_params(1),
        name="ret_sample",
    )(x.reshape(n, d), cos, sin, state, g, win, wout, *earlier)
    return out.reshape(ns, seg, d), s_new


def _pool_mixer(x_ref, nm_ref, wp_ref, ps_ref, xp_ref, bo_ref, pos_start):
    nseg, lt, d = x_ref.shape
    gw = d // len(POOL_WINDOWS)
    x = x_ref[...]
    h = _rms(x, nm_ref[...])
    xp_ref[:, POOL_HIST:, :] = h
    pos = pos_start + lax.broadcasted_iota(jnp.int32, (1, lt, 1), 1)
    ys = []
    for gi, w in enumerate(POOL_WINDOWS):
        cols = slice(gi * gw, (gi + 1) * gw)
        acc = xp_ref[:, :, cols]
        span = 1
        while span < w:
            acc = acc + pltpu.roll(acc, span, axis=1)
            span *= 2
        cnt = jnp.minimum(pos + 1, w).astype(F32)
        dg = (acc[:, POOL_HIST:, :] / cnt - h[:, :, cols]).astype(BF16).reshape(nseg * lt, gw)
        ys.append(_dot(dg, wp_ref[gi]))
    y = jnp.concatenate(ys, axis=-1) * ps_ref[...]
    tail = xp_ref[:, lt:lt + POOL_HIST, :]
    bo_ref[...] = tail
    xp_ref[:, 0:POOL_HIST, :] = tail
    return x.reshape(nseg * lt, d) + y


def _pool_ffn_kernel(x_ref, buf_ref, nm_ref, wp_ref, ps_ref, nf_ref, win_ref, wout_ref,
                     nfin_ref, xo_ref, bo_ref, xp_ref, act_ref, *, pos0, final):
    nseg, lt, d = x_ref.shape
    j = pl.program_id(1)

    @pl.when(j == 0)
    def _():
        xp_ref[:, 0:POOL_HIST, :] = buf_ref[...]

    x1 = _pool_mixer(x_ref, nm_ref, wp_ref, ps_ref, xp_ref, bo_ref, pos0 + j * lt)
    x2 = _ffn(x1, nf_ref[...], win_ref, wout_ref, act_ref)
    if final:
        x2 = _rms(x2, nfin_ref[...])
    xo_ref[...] = x2.reshape(nseg, lt, d)


def _pool_ffn_call(x, buf16, nm, wp, ps, nf, win, wout, nfin, *, mix_layer, pool_layer,
                   seg_block, tile, pos0, final, cast_weights=()):
    ns, l, d = x.shape
    grid = (ns // seg_block, l // tile)
    casts = _Casts(cast_weights, grid)
    return pl.pallas_call(
        casts.wrap(functools.partial(_pool_ffn_kernel, pos0=pos0, final=final), 9, 2),
        grid=grid,
        in_specs=[pl.BlockSpec((seg_block, tile, d), lambda i, j: (i, j, 0)),
                  pl.BlockSpec((seg_block, POOL_HIST, d), lambda i, j: (i, 0, 0)),
                  _layer_spec(nm.shape, mix_layer), _layer_spec(wp.shape, pool_layer),
                  _layer_spec(ps.shape, pool_layer), _layer_spec(nf.shape, mix_layer),
                  _const_spec(win.shape), _const_spec(wout.shape),
                  _const_spec(nfin.shape)] + casts.in_specs,
        out_specs=[pl.BlockSpec((seg_block, tile, d), lambda i, j: (i, j, 0)),
                   pl.BlockSpec((seg_block, POOL_HIST, d), lambda i, j: (i, 0, 0))]
        + casts.out_specs,
        out_shape=[jax.ShapeDtypeStruct((ns, l, d), F32),
                   jax.ShapeDtypeStruct((ns, POOL_HIST, d), F32)] + casts.out_shape,
        scratch_shapes=[pltpu.VMEM((seg_block, POOL_HIST + tile, d), F32),
                        pltpu.VMEM((seg_block * tile, wout.shape[0]), BF16)],
        compiler_params=_params(2),
        name="pool_ffn",
    )(x, buf16, nm, wp, ps, nf, win, wout, nfin, *casts.operands)


def _rope_angles(pos, half):
    inv = ROPE_BASE ** (-np.arange(half, dtype=np.float64) / half)
    return np.asarray(pos, np.float64)[:, None] * inv[None, :]


def _cos_sin(ang):
    return jnp.asarray(np.cos(ang), F32), jnp.asarray(np.sin(ang), F32)


def kernel(x_prompt, x_sample, state_ret, state_pool, norm_mix, w_ret_in, w_ret_out, w_pool,
           pool_scale, norm_ffn, w_ffn_in, w_ffn_out, norm_final):
    bp, lp, d = x_prompt.shape
    bs, ls, _ = x_sample.shape
    depth = norm_mix.shape[0]
    n_ret, _, nh, dk, dv = state_ret.shape
    assert nh == RET_HEADS and state_pool.shape[2] == POOL_HIST - 1
    assert lp % PROMPT_TILE == 0 and PROMPT_TILE % RET_CHUNK == 0 and ls >= POOL_HIST
    assert (bp * lp) % FFN_TILE == 0
    assert depth % 2 == 0

    ret_w = (w_ret_in[0].astype(BF16), w_ret_out[0].astype(BF16))
    w_pool_b = w_pool.astype(BF16)
    norm_mix3 = norm_mix[:, None, :]
    norm_ffn3 = norm_ffn[:, None, :]
    pool_scale3 = pool_scale[:, None, :]
    nfin = norm_final[None, :]

    half = dk // 2
    cosr, sinr = _cos_sin(_rope_angles(np.arange(PROMPT_TILE), half))
    cosb, sinb = _cos_sin(_rope_angles(np.arange(0, lp, PROMPT_TILE), half)[:, None, :])
    rope_p = (cosr, sinr, cosb, sinb)
    cos_s, sin_s = _cos_sin(np.tile(_rope_angles(PAST_LEN + np.arange(ls), half), (bs, 1)))

    pool0_p = jnp.zeros((bp, POOL_HIST, d), state_pool.dtype)
    pool0_s = jnp.pad(state_pool, ((0, 0), (0, 0), (1, 0), (0, 0)))

    xp, xs = x_prompt, x_sample
    ret_p, ret_s, pool_p, pool_s = [], [], [], []
    def ffn_weights(layer):
        return [(w_ffn_in, layer), (w_ffn_out, layer)]

    for i in range(depth):
        jm = i // 2
        if i % 2 == 0:
            xp, sp, *ffn_w = _ret_prompt_call(xp, rope_p, (nh, dk, dv), norm_mix3, *ret_w, i,
                                              cast_weights=ffn_weights(i))
            earlier = list(ret_s) if jm == n_ret - 1 else []
            xs, ss = _ret_sample_call(xs, cos_s, sin_s, state_ret, norm_mix3, *ret_w, i, jm,
                                      earlier)
            ret_s.append(ss)
            ret_p.append(sp)
            xp, *next_ffn_w = _ffn_call(xp.reshape(bp * lp, d), norm_ffn3, *ffn_w, i, FFN_TILE,
                                        cast_weights=ffn_weights(i + 1))
            xp = xp.reshape(bp, lp, d)
            xs, = _ffn_call(xs.reshape(bs * ls, d), norm_ffn3, *ffn_w, i, bs * ls)
            xs = xs.reshape(bs, ls, d)
            ffn_w = next_ffn_w
        else:
            kw = dict(mix_layer=i, pool_layer=jm, final=i == depth - 1)
            args = (norm_mix3, w_pool_b, pool_scale3, norm_ffn3, *ffn_w, nfin)
            next_ret = [(w_ret_in, jm + 1), (w_ret_out, jm + 1)] if i + 1 < depth else []
            xp, bpn, *ret_w = _pool_ffn_call(xp, pool0_p, *args, seg_block=1, tile=PROMPT_TILE,
                                             pos0=0, cast_weights=next_ret, **kw)
            xs, bsn = _pool_ffn_call(xs, pool0_s[jm], *args, seg_block=bs, tile=ls, pos0=PAST_LEN,
                                     **kw)
            pool_p.append(bpn[:, 1:, :])
            pool_s.append(bsn[:, 1:, :])
    ret_s_stack = ret_s[-1] if n_ret > 1 else ret_s[0][None]
    return (xp, xs, jnp.stack(ret_p), ret_s_stack, jnp.stack(pool_p), jnp.stack(pool_s))
```

```python
import functools
import math

import numpy as np

import jax
import jax.numpy as jnp
from jax import lax
from jax.experimental import pallas as pl
from jax.experimental.pallas import tpu as pltpu

F32 = jnp.float32
BF16 = jnp.bfloat16

EPS = 1e-6
ROPE_BASE = 10000.0
PAST_LEN = 1024
RET_HEADS = 4
POOL_WINDOWS = (2, 4, 8, 16)
POOL_HIST = 16
PROMPT_TILE = 512
RET_CHUNK = 256
RET_TILES_PER_STEP = 2
FFN_TILE = 1024
FFN_CHUNK = 256
BF16_ROWS = 16
VMEM_LIMIT_BYTES = 56 * 1024 * 1024


def _log_gamma(head):
    return math.log1p(-(2.0 ** (-5.0 - head)))


def _rms(x, g):
    return (x * lax.rsqrt(jnp.mean(x * x, axis=-1, keepdims=True) + EPS)) * g


def _dot(a, b):
    return jnp.dot(a, b, preferred_element_type=F32)


def _dot_nt(a, b):
    return lax.dot_general(a, b, (((1,), (1,)), ((), ())), preferred_element_type=F32)


def _dot_tn(a, b):
    return lax.dot_general(a, b, (((0,), (0,)), ((), ())), preferred_element_type=F32)


def _rotary(t, cos, sin):
    half = t.shape[-1] // 2
    t1, t2 = t[:, :half], t[:, half:]
    return jnp.concatenate([t1 * cos - t2 * sin, t1 * sin + t2 * cos], axis=-1)


def _silu(t):
    return t * jax.nn.sigmoid(t)


def _decay_mask(lg, n):
    r = lax.broadcasted_iota(jnp.int32, (n, n), 0)
    c = lax.broadcasted_iota(jnp.int32, (n, n), 1)
    diff = (r - c).astype(F32)
    return jnp.where(diff >= 0, jnp.exp(lg * jnp.maximum(diff, 0.0)), 0.0)


def _ffn(x, nf, win_ref, wout_ref, act_ref):
    d_ff = wout_ref.shape[0]
    h = _rms(x, nf).astype(BF16)
    for lo in range(0, d_ff, FFN_CHUNK):
        up = _dot(h, win_ref[:, d_ff + lo:d_ff + lo + FFN_CHUNK])
        act_ref[:, lo:lo + FFN_CHUNK] = (
            _silu(_dot(h, win_ref[:, lo:lo + FFN_CHUNK])) * up).astype(BF16)
    return x + _dot(act_ref[...], wout_ref[...])


def _const_spec(shape):
    nd = len(shape)
    return pl.BlockSpec(shape, lambda *_: (0,) * nd, pipeline_mode=pl.Buffered(1))


def _layer_spec(stacked_shape, layer):
    nd = len(stacked_shape)
    return pl.BlockSpec((None,) + tuple(stacked_shape[1:]),
                        lambda *_: (layer,) + (0,) * (nd - 1), pipeline_mode=pl.Buffered(1))


def _params(n_axes):
    return pltpu.CompilerParams(dimension_semantics=("arbitrary",) * n_axes,
                                vmem_limit_bytes=VMEM_LIMIT_BYTES)


class _Casts:
    def __init__(self, weights, grid):
        self.grid = grid
        self.in_specs, self.out_specs, self.out_shape, self.operands, self.n_blocks = [], [], [], [], []
        n_steps = math.prod(grid)
        for w, layer in weights:
            _, rows, cols = w.shape
            blk = next(b for b in range(BF16_ROWS, rows + 1, BF16_ROWS)
                       if rows % b == 0 and rows // b <= n_steps)
            n_blk = rows // blk
            self.in_specs.append(pl.BlockSpec(
                (None, blk, cols),
                lambda *g, n_blk=n_blk, layer=layer: (layer, jnp.minimum(self.step(*g), n_blk - 1), 0)))
            self.out_specs.append(pl.BlockSpec(
                (blk, cols), lambda *g, n_blk=n_blk: (jnp.minimum(self.step(*g), n_blk - 1), 0)))
            self.out_shape.append(jax.ShapeDtypeStruct((rows, cols), BF16))
            self.operands.append(w)
            self.n_blocks.append(n_blk)

    def __len__(self):
        return len(self.operands)

    def step(self, *g):
        flat = g[0]
        for idx, extent in zip(g[1:], self.grid[1:]):
            flat = flat * extent + idx
        return flat

    def wrap(self, body, n_in, n_out):
        n = len(self)

        def kernel(*refs):
            ins, srcs = refs[:n_in], refs[n_in:n_in + n]
            outs = refs[n_in + n:n_in + n + n_out]
            dsts = refs[n_in + n + n_out:n_in + 2 * n + n_out]
            body(*ins, *outs, *refs[n_in + 2 * n + n_out:])
            step = self.step(*(pl.program_id(a) for a in range(len(self.grid))))
            for src, dst, n_blk in zip(srcs, dsts, self.n_blocks):
                @pl.when(step < n_blk)
                def _():
                    dst[...] = src[...].astype(BF16)

        return kernel


def _ffn_kernel(x_ref, nf_ref, win_ref, wout_ref, o_ref, act_ref):
    o_ref[...] = _ffn(x_ref[...], nf_ref[...], win_ref, wout_ref, act_ref)


def _ffn_call(x2d, nf, win, wout, layer, tile, cast_weights=()):
    n, d = x2d.shape
    grid = (n // tile,)
    casts = _Casts(cast_weights, grid)
    return pl.pallas_call(
        casts.wrap(_ffn_kernel, 4, 1),
        grid=grid,
        in_specs=[pl.BlockSpec((tile, d), lambda i: (i, 0)), _layer_spec(nf.shape, layer),
                  _const_spec(win.shape), _const_spec(wout.shape)] + casts.in_specs,
        out_specs=[pl.BlockSpec((tile, d), lambda i: (i, 0))] + casts.out_specs,
        out_shape=[jax.ShapeDtypeStruct((n, d), F32)] + casts.out_shape,
        scratch_shapes=[pltpu.VMEM((tile, wout.shape[0]), BF16)],
        compiler_params=_params(1),
        name="ffn",
    )(x2d, nf, win, wout, *casts.operands)


def _ret_head_proj(h, win_ref, hd, dk, dv):
    qk = RET_HEADS * dk
    vd = RET_HEADS * dv
    q = _dot(h, win_ref[:, hd * dk:(hd + 1) * dk])
    k = _dot(h, win_ref[:, qk + hd * dk:qk + (hd + 1) * dk]) * (dk ** -0.5)
    v = _dot(h, win_ref[:, 2 * qk + hd * dv:2 * qk + (hd + 1) * dv])
    gate = _dot(h, win_ref[:, 2 * qk + vd + hd * dv:2 * qk + vd + (hd + 1) * dv])
    return q, k, v, gate


def _ret_prompt_kernel(x_ref, cosr_ref, sinr_ref, cosb_ref, sinb_ref, g_ref, win_ref, wout_ref,
                       xo_ref, s_ref, dmask_ref, qdec_ref, kdec_ref, og_ref):
    t = og_ref.shape[0]
    c = dmask_ref.shape[1]
    dk = s_ref.shape[2]
    j = pl.program_id(1)

    @pl.when((pl.program_id(0) == 0) & (j == 0))
    def _():
        row = lax.broadcasted_iota(jnp.int32, (c, dk), 0).astype(F32)
        col = lax.broadcasted_iota(jnp.int32, (dk, c), 1).astype(F32)
        for hd in range(RET_HEADS):
            lg = _log_gamma(hd)
            dmask_ref[hd] = _decay_mask(lg, c)
            qdec_ref[hd] = jnp.exp(lg * (row + 1.0))
            kdec_ref[hd] = jnp.exp(lg * (c - 1.0 - col))

    @pl.when(j == 0)
    def _():
        s_ref[...] = jnp.zeros(s_ref.shape, F32)

    for sub in range(x_ref.shape[1] // t):
        rows = slice(sub * t, (sub + 1) * t)
        xo_ref[0, rows] = _ret_prompt_tile(
            x_ref[0, rows], cosb_ref[sub], sinb_ref[sub], cosr_ref, sinr_ref, g_ref, win_ref,
            wout_ref, s_ref, dmask_ref, qdec_ref, kdec_ref, og_ref)


def _ret_prompt_tile(x, cosb, sinb, cosr_ref, sinr_ref, g_ref, win_ref, wout_ref,
                     s_ref, dmask_ref, qdec_ref, kdec_ref, og_ref):
    t = x.shape[0]
    c = dmask_ref.shape[1]
    dk, dv = s_ref.shape[2], s_ref.shape[3]
    h = _rms(x, g_ref[...]).astype(BF16)
    cosr, sinr = cosr_ref[...], sinr_ref[...]
    cos = cosb * cosr - sinb * sinr
    sin = sinb * cosr + cosb * sinr
    qk, vd = RET_HEADS * dk, RET_HEADS * dv
    n_chunks = t // c
    proj = [dict() for _ in range(RET_HEADS)]
    out = [x]

    def proj_steps(hd):
        def q_step():
            proj[hd]["q"] = _rotary(_dot(h, win_ref[:, hd * dk:(hd + 1) * dk]), cos, sin)

        def k_step():
            k = _dot(h, win_ref[:, qk + hd * dk:qk + (hd + 1) * dk]) * (dk ** -0.5)
            proj[hd]["k_t"] = _rotary(k, cos, sin).T

        def v_step():
            lo = 2 * qk + hd * dv
            proj[hd]["v"] = _dot(h, win_ref[:, lo:lo + dv]).astype(BF16)

        def gate_step():
            lo = 2 * qk + vd + hd * dv
            proj[hd]["gate"] = _dot(h, win_ref[:, lo:lo + dv])

        return [q_step, k_step, v_step, gate_step]

    def out_step(hd):
        def step():
            out[0] = out[0] + _dot(og_ref[:, hd * dv:(hd + 1) * dv],
                                   wout_ref[hd * dv:(hd + 1) * dv, :])
        return step

    def core(hd, fillers):
        lg = _log_gamma(hd)
        q, k_t, v, gate = (proj[hd][name] for name in ("q", "k_t", "v", "gate"))
        chunks = [slice(ci * c, (ci + 1) * c) for ci in range(n_chunks)]
        scores = [_dot(q[rows].astype(BF16), k_t[:, rows].astype(BF16)) for rows in chunks]
        fillers.pop(0)()
        s_cur = s_ref[0, hd]
        for rows, sc in zip(chunks, scores):
            vc = v[rows]
            q_dec = (q[rows] * qdec_ref[hd]).astype(BF16)
            k_dec_t = (k_t[:, rows] * kdec_ref[hd]).astype(BF16)
            lhs = jnp.concatenate([(sc * dmask_ref[hd]).astype(BF16), q_dec], axis=1)
            rhs = jnp.concatenate([vc, s_cur.astype(BF16)], axis=0)
            o = _dot(lhs, rhs)
            s_cur = math.exp(lg * c) * s_cur + _dot(k_dec_t, vc)
            if fillers:
                fillers.pop(0)()
            o = o * lax.rsqrt(jnp.mean(o * o, axis=-1, keepdims=True) + EPS)
            og_ref[rows, hd * dv:(hd + 1) * dv] = (_silu(gate[rows]) * o).astype(BF16)
        s_ref[0, hd] = s_cur
        for filler in fillers:
            filler()

    for step in proj_steps(0):
        step()
    for hd in range(RET_HEADS):
        if hd + 1 < RET_HEADS:
            fillers = proj_steps(hd + 1)
        else:
            fillers = [out_step(i) for i in range(hd)]
        core(hd, fillers)
    out_step(RET_HEADS - 1)()
    return out[0]


def _ret_prompt_call(x, rope, s_shape, g, win, wout, mix_layer, cast_weights=()):
    b, l, d = x.shape
    nh, dk, dv = s_shape
    t, c = PROMPT_TILE, RET_CHUNK
    cosr, sinr, cosb, sinb = rope
    half = cosr.shape[1]
    n_sub = RET_TILES_PER_STEP
    base_spec = pl.BlockSpec((n_sub, 1, half), lambda i, j: (j, 0, 0))
    grid = (b, l // (n_sub * t))
    casts = _Casts(cast_weights, grid)
    return pl.pallas_call(
        casts.wrap(_ret_prompt_kernel, 8, 2),
        grid=grid,
        in_specs=[pl.BlockSpec((1, n_sub * t, d), lambda i, j: (i, j, 0)),
                  _const_spec(cosr.shape), _const_spec(sinr.shape), base_spec, base_spec,
                  _layer_spec(g.shape, mix_layer), _const_spec(win.shape),
                  _const_spec(wout.shape)] + casts.in_specs,
        out_specs=[pl.BlockSpec((1, n_sub * t, d), lambda i, j: (i, j, 0)),
                   pl.BlockSpec((1, nh, dk, dv), lambda i, j: (i, 0, 0, 0))] + casts.out_specs,
        out_shape=[jax.ShapeDtypeStruct((b, l, d), F32),
                   jax.ShapeDtypeStruct((b, nh, dk, dv), F32)] + casts.out_shape,
        scratch_shapes=[pltpu.VMEM((nh, c, c), F32), pltpu.VMEM((nh, c, dk), F32),
                        pltpu.VMEM((nh, dk, c), F32), pltpu.VMEM((t, nh * dv), BF16)],
        compiler_params=_params(2),
        name="ret_prompt",
    )(x, cosr, sinr, cosb, sinb, g, win, wout, *casts.operands)


def _ret_sample_kernel(x_ref, cos_ref, sin_ref, s0_ref, g_ref, win_ref, wout_ref, *rest, seg):
    earlier = rest[:-7]
    xo_ref, s_ref, q_ref, k_ref, v_ref, gate_ref, og_ref = rest[-7:]
    if earlier:
        for layer, prev_ref in enumerate(earlier):
            s_ref[layer] = prev_ref[...]
        s_ref = s_ref.at[len(earlier)]
    dk, dv = s_ref.shape[2], s_ref.shape[3]
    s = pl.program_id(0)

    @pl.when(s == 0)
    def _():
        h = _rms(x_ref[...], g_ref[...]).astype(BF16)
        cos, sin = cos_ref[...], sin_ref[...]
        for hd in range(RET_HEADS):
            q, k, v, gate = _ret_head_proj(h, win_ref, hd, dk, dv)
            q_ref[:, hd * dk:(hd + 1) * dk] = _rotary(q, cos, sin)
            k_ref[:, hd * dk:(hd + 1) * dk] = _rotary(k, cos, sin)
            v_ref[:, hd * dv:(hd + 1) * dv] = v
            gate_ref[:, hd * dv:(hd + 1) * dv] = gate

    rows = pl.ds(pl.multiple_of(s * seg, seg), seg)
    idx = lax.broadcasted_iota(jnp.int32, (seg, 1), 0).astype(F32)
    for hd in range(RET_HEADS):
        lg = _log_gamma(hd)
        q = q_ref[rows, hd * dk:(hd + 1) * dk]
        k = k_ref[rows, hd * dk:(hd + 1) * dk]
        v = v_ref[rows, hd * dv:(hd + 1) * dv].astype(BF16)
        gate = gate_ref[rows, hd * dv:(hd + 1) * dv]
        q_dec = (q * jnp.exp(lg * (idx + 1.0))).astype(BF16)
        k_dec = (k * jnp.exp(lg * (seg - 1.0 - idx))).astype(BF16)
        scores = _dot_nt(q.astype(BF16), k.astype(BF16)) * _decay_mask(lg, seg)
        s_old = s0_ref[0, hd]
        o = _dot(scores.astype(BF16), v) + _dot(q_dec, s_old.astype(BF16))
        s_ref[0, hd] = math.exp(lg * seg) * s_old + _dot_tn(k_dec, v)
        o = o * lax.rsqrt(jnp.mean(o * o, axis=-1, keepdims=True) + EPS)
        og_ref[rows, hd * dv:(hd + 1) * dv] = _silu(gate) * o

    @pl.when(s == pl.num_programs(0) - 1)
    def _():
        xo_ref[...] = x_ref[...] + _dot(og_ref[...].astype(BF16), wout_ref[...])


def _ret_sample_call(x, cos, sin, state, g, win, wout, mix_layer, ret_layer, earlier):
    ns, seg, d = x.shape
    _, _, nh, dk, dv = state.shape
    n = ns * seg
    stream_spec = pl.BlockSpec((1, nh, dk, dv), lambda i: (i, 0, 0, 0))
    if earlier:
        n_out = len(earlier) + 1
        out_state = jax.ShapeDtypeStruct((n_out, ns, nh, dk, dv), F32)
        out_state_spec = pl.BlockSpec((n_out, 1, nh, dk, dv), lambda i: (0, i, 0, 0, 0))
    else:
        out_state = jax.ShapeDtypeStruct((ns, nh, dk, dv), F32)
        out_state_spec = stream_spec
    out, s_new = pl.pallas_call(
        functools.partial(_ret_sample_kernel, seg=seg),
        grid=(ns,),
        in_specs=[_const_spec((n, d)), _const_spec(cos.shape), _const_spec(sin.shape),
                  pl.BlockSpec((None, 1, nh, dk, dv), lambda i: (ret_layer, i, 0, 0, 0)),
                  _layer_spec(g.shape, mix_layer), _const_spec(win.shape),
                  _const_spec(wout.shape)] + [stream_spec] * len(earlier),
        out_specs=[pl.BlockSpec((n, d), lambda i: (0, 0)), out_state_spec],
        out_shape=[jax.ShapeDtypeStruct((n, d), F32), out_state],
        scratch_shapes=[pltpu.VMEM((n, nh * dk), F32), pltpu.VMEM((n, nh * dk), F32),
                        pltpu.VMEM((n, nh * dv), F32), pltpu.VMEM((n, nh * dv), F32),
                        pltpu.VMEM((n, nh * dv), F32)],
        compiler_params=_params(1),
        name="ret_sample",
    )(x.reshape(n, d), cos, sin, state, g, win, wout, *earlier)
    return out.reshape(ns, seg, d), s_new


def _pool_mixer(x_ref, nm_ref, wp_ref, ps_ref, xp_ref, bo_ref, pos_start):
    nseg, lt, d = x_ref.shape
    gw = d // len(POOL_WINDOWS)
    x = x_ref[...]
    h = _rms(x, nm_ref[...])
    xp_ref[:, POOL_HIST:, :] = h
    pos = pos_start + lax.broadcasted_iota(jnp.int32, (1, lt, 1), 1)
    ys = []
    for gi, w in enumerate(POOL_WINDOWS):
        cols = slice(gi * gw, (gi + 1) * gw)
        acc = xp_ref[:, :, cols]
        span = 1
        while span < w:
            acc = acc + pltpu.roll(acc, span, axis=1)
            span *= 2
        cnt = jnp.minimum(pos + 1, w).astype(F32)
        dg = (acc[:, POOL_HIST:, :] / cnt - h[:, :, cols]).astype(BF16).reshape(nseg * lt, gw)
        ys.append(_dot(dg, wp_ref[gi]))
    y = jnp.concatenate(ys, axis=-1) * ps_ref[...]
    tail = xp_ref[:, lt:lt + POOL_HIST, :]
    bo_ref[...] = tail
    xp_ref[:, 0:POOL_HIST, :] = tail
    return x.reshape(nseg * lt, d) + y


def _pool_ffn_kernel(x_ref, buf_ref, nm_ref, wp_ref, ps_ref, nf_ref, win_ref, wout_ref,
                     nfin_ref, xo_ref, bo_ref, xp_ref, act_ref, *, pos0, final):
    nseg, lt, d = x_ref.shape
    j = pl.program_id(1)

    @pl.when(j == 0)
    def _():
        xp_ref[:, 0:POOL_HIST, :] = buf_ref[...]

    x1 = _pool_mixer(x_ref, nm_ref, wp_ref, ps_ref, xp_ref, bo_ref, pos0 + j * lt)
    x2 = _ffn(x1, nf_ref[...], win_ref, wout_ref, act_ref)
    if final:
        x2 = _rms(x2, nfin_ref[...])
    xo_ref[...] = x2.reshape(nseg, lt, d)


def _pool_ffn_call(x, buf16, nm, wp, ps, nf, win, wout, nfin, *, mix_layer, pool_layer,
                   seg_block, tile, pos0, final, cast_weights=()):
    ns, l, d = x.shape
    grid = (ns // seg_block, l // tile)
    casts = _Casts(cast_weights, grid)
    return pl.pallas_call(
        casts.wrap(functools.partial(_pool_ffn_kernel, pos0=pos0, final=final), 9, 2),
        grid=grid,
        in_specs=[pl.BlockSpec((seg_block, tile, d), lambda i, j: (i, j, 0)),
                  pl.BlockSpec((seg_block, POOL_HIST, d), lambda i, j: (i, 0, 0)),
                  _layer_spec(nm.shape, mix_layer), _layer_spec(wp.shape, pool_layer),
                  _layer_spec(ps.shape, pool_layer), _layer_spec(nf.shape, mix_layer),
                  _const_spec(win.shape), _const_spec(wout.shape),
                  _const_spec(nfin.shape)] + casts.in_specs,
        out_specs=[pl.BlockSpec((seg_block, tile, d), lambda i, j: (i, j, 0)),
                   pl.BlockSpec((seg_block, POOL_HIST, d), lambda i, j: (i, 0, 0))]
        + casts.out_specs,
        out_shape=[jax.ShapeDtypeStruct((ns, l, d), F32),
                   jax.ShapeDtypeStruct((ns, POOL_HIST, d), F32)] + casts.out_shape,
        scratch_shapes=[pltpu.VMEM((seg_block, POOL_HIST + tile, d), F32),
                        pltpu.VMEM((seg_block * tile, wout.shape[0]), BF16)],
        compiler_params=_params(2),
        name="pool_ffn",
    )(x, buf16, nm, wp, ps, nf, win, wout, nfin, *casts.operands)


def _rope_angles(pos, half):
    inv = ROPE_BASE ** (-np.arange(half, dtype=np.float64) / half)
    return np.asarray(pos, np.float64)[:, None] * inv[None, :]


def _cos_sin(ang):
    return jnp.asarray(np.cos(ang), F32), jnp.asarray(np.sin(ang), F32)


def kernel(x_prompt, x_sample, state_ret, state_pool, norm_mix, w_ret_in, w_ret_out, w_pool,
           pool_scale, norm_ffn, w_ffn_in, w_ffn_out, norm_final):
    bp, lp, d = x_prompt.shape
    bs, ls, _ = x_sample.shape
    depth = norm_mix.shape[0]
    n_ret, _, nh, dk, dv = state_ret.shape
    assert nh == RET_HEADS and state_pool.shape[2] == POOL_HIST - 1
    assert lp % (PROMPT_TILE * RET_TILES_PER_STEP) == 0 and PROMPT_TILE % RET_CHUNK == 0
    assert ls >= POOL_HIST
    assert (bp * lp) % FFN_TILE == 0
    assert depth % 2 == 0

    ret_w = (w_ret_in[0].astype(BF16), w_ret_out[0].astype(BF16))
    w_pool_b = w_pool.astype(BF16)
    norm_mix3 = norm_mix[:, None, :]
    norm_ffn3 = norm_ffn[:, None, :]
    pool_scale3 = pool_scale[:, None, :]
    nfin = norm_final[None, :]

    half = dk // 2
    cosr, sinr = _cos_sin(_rope_angles(np.arange(PROMPT_TILE), half))
    cosb, sinb = _cos_sin(_rope_angles(np.arange(0, lp, PROMPT_TILE), half)[:, None, :])
    rope_p = (cosr, sinr, cosb, sinb)
    cos_s, sin_s = _cos_sin(np.tile(_rope_angles(PAST_LEN + np.arange(ls), half), (bs, 1)))

    pool0_p = jnp.zeros((bp, POOL_HIST, d), state_pool.dtype)
    pool0_s = jnp.pad(state_pool, ((0, 0), (0, 0), (1, 0), (0, 0)))

    xp, xs = x_prompt, x_sample
    ret_p, ret_s, pool_p, pool_s = [], [], [], []
    def ffn_weights(layer):
        return [(w_ffn_in, layer), (w_ffn_out, layer)]

    for i in range(depth):
        jm = i // 2
        if i % 2 == 0:
            xp, sp, *ffn_w = _ret_prompt_call(xp, rope_p, (nh, dk, dv), norm_mix3, *ret_w, i,
                                              cast_weights=ffn_weights(i))
            earlier = list(ret_s) if jm == n_ret - 1 else []
            xs, ss = _ret_sample_call(xs, cos_s, sin_s, state_ret, norm_mix3, *ret_w, i, jm,
                                      earlier)
            ret_s.append(ss)
            ret_p.append(sp)
            xp, *next_ffn_w = _ffn_call(xp.reshape(bp * lp, d), norm_ffn3, *ffn_w, i, FFN_TILE,
                                        cast_weights=ffn_weights(i + 1))
            xp = xp.reshape(bp, lp, d)
            xs, = _ffn_call(xs.reshape(bs * ls, d), norm_ffn3, *ffn_w, i, bs * ls)
            xs = xs.reshape(bs, ls, d)
            ffn_w = next_ffn_w
        else:
            kw = dict(mix_layer=i, pool_layer=jm, final=i == depth - 1)
            args = (norm_mix3, w_pool_b, pool_scale3, norm_ffn3, *ffn_w, nfin)
            next_ret = [(w_ret_in, jm + 1), (w_ret_out, jm + 1)] if i + 1 < depth else []
            xp, bpn, *ret_w = _pool_ffn_call(xp, pool0_p, *args, seg_block=1, tile=PROMPT_TILE,
                                             pos0=0, cast_weights=next_ret, **kw)
            xs, bsn = _pool_ffn_call(xs, pool0_s[jm], *args, seg_block=bs, tile=ls, pos0=PAST_LEN,
                                     **kw)
            pool_p.append(bpn[:, 1:, :])
            pool_s.append(bsn[:, 1:, :])
    ret_s_stack = ret_s[-1] if n_ret > 1 else ret_s[0][None]
    return (xp, xs, jnp.stack(ret_p), ret_s_stack, jnp.stack(pool_p), jnp.stack(pool_s))
```

```python
import functools
import math

import numpy as np

import jax
import jax.numpy as jnp
from jax import lax
from jax.experimental import pallas as pl
from jax.experimental.pallas import tpu as pltpu

F32 = jnp.float32
BF16 = jnp.bfloat16

EPS = 1e-6
ROPE_BASE = 10000.0
PAST_LEN = 1024
RET_HEADS = 4
POOL_WINDOWS = (2, 4, 8, 16)
POOL_HIST = 16
PROMPT_TILE = 512
RET_CHUNK = 256
RET_TILES_PER_STEP = 2
FFN_TILE = 1024
FFN_CHUNK = 256
BF16_ROWS = 16
VMEM_LIMIT_BYTES = 56 * 1024 * 1024


def _log_gamma(head):
    return math.log1p(-(2.0 ** (-5.0 - head)))


def _rms(x, g):
    return (x * lax.rsqrt(jnp.mean(x * x, axis=-1, keepdims=True) + EPS)) * g


def _dot(a, b):
    return jnp.dot(a, b, preferred_element_type=F32)


def _dot_nt(a, b):
    return lax.dot_general(a, b, (((1,), (1,)), ((), ())), preferred_element_type=F32)


def _dot_tn(a, b):
    return lax.dot_general(a, b, (((0,), (0,)), ((), ())), preferred_element_type=F32)


def _rotary(t, cos, sin):
    half = t.shape[-1] // 2
    t1, t2 = t[:, :half], t[:, half:]
    return jnp.concatenate([t1 * cos - t2 * sin, t1 * sin + t2 * cos], axis=-1)


def _silu(t):
    return t * jax.nn.sigmoid(t)


def _decay_mask(lg, n):
    r = lax.broadcasted_iota(jnp.int32, (n, n), 0)
    c = lax.broadcasted_iota(jnp.int32, (n, n), 1)
    diff = (r - c).astype(F32)
    return jnp.where(diff >= 0, jnp.exp(lg * jnp.maximum(diff, 0.0)), 0.0)


def _ffn(x, nf, win_ref, wout_ref, act_ref):
    d_ff = wout_ref.shape[0]
    h = _rms(x, nf).astype(BF16)
    for lo in range(0, d_ff, FFN_CHUNK):
        up = _dot(h, win_ref[:, d_ff + lo:d_ff + lo + FFN_CHUNK])
        act_ref[:, lo:lo + FFN_CHUNK] = (
            _silu(_dot(h, win_ref[:, lo:lo + FFN_CHUNK])) * up).astype(BF16)
    return x + _dot(act_ref[...], wout_ref[...])


def _const_spec(shape):
    nd = len(shape)
    return pl.BlockSpec(shape, lambda *_: (0,) * nd, pipeline_mode=pl.Buffered(1))


def _layer_spec(stacked_shape, layer):
    nd = len(stacked_shape)
    return pl.BlockSpec((None,) + tuple(stacked_shape[1:]),
                        lambda *_: (layer,) + (0,) * (nd - 1), pipeline_mode=pl.Buffered(1))


def _params(n_axes):
    return pltpu.CompilerParams(dimension_semantics=("arbitrary",) * n_axes,
                                vmem_limit_bytes=VMEM_LIMIT_BYTES)


class _Casts:
    def __init__(self, weights, grid):
        self.grid = grid
        self.in_specs, self.out_specs, self.out_shape, self.operands, self.n_blocks = [], [], [], [], []
        n_steps = math.prod(grid)
        for w, layer in weights:
            _, rows, cols = w.shape
            blk = next(b for b in range(BF16_ROWS, rows + 1, BF16_ROWS)
                       if rows % b == 0 and rows // b <= n_steps)
            n_blk = rows // blk
            self.in_specs.append(pl.BlockSpec(
                (None, blk, cols),
                lambda *g, n_blk=n_blk, layer=layer: (layer, jnp.minimum(self.step(*g), n_blk - 1), 0)))
            self.out_specs.append(pl.BlockSpec(
                (blk, cols), lambda *g, n_blk=n_blk: (jnp.minimum(self.step(*g), n_blk - 1), 0)))
            self.out_shape.append(jax.ShapeDtypeStruct((rows, cols), BF16))
            self.operands.append(w)
            self.n_blocks.append(n_blk)

    def __len__(self):
        return len(self.operands)

    def step(self, *g):
        flat = g[0]
        for idx, extent in zip(g[1:], self.grid[1:]):
            flat = flat * extent + idx
        return flat

    def wrap(self, body, n_in, n_out):
        n = len(self)

        def kernel(*refs):
            ins, srcs = refs[:n_in], refs[n_in:n_in + n]
            outs = refs[n_in + n:n_in + n + n_out]
            dsts = refs[n_in + n + n_out:n_in + 2 * n + n_out]
            body(*ins, *outs, *refs[n_in + 2 * n + n_out:])
            step = self.step(*(pl.program_id(a) for a in range(len(self.grid))))
            for src, dst, n_blk in zip(srcs, dsts, self.n_blocks):
                @pl.when(step < n_blk)
                def _():
                    dst[...] = src[...].astype(BF16)

        return kernel


def _ffn_kernel(xp_ref, xs_ref, nf_ref, win_ref, wout_ref, op_ref, os_ref, act_ref):
    i = pl.program_id(0)
    last = pl.num_programs(0) - 1

    @pl.when(i < last)
    def _():
        op_ref[...] = _ffn(xp_ref[...], nf_ref[...], win_ref, wout_ref, act_ref)

    @pl.when(i == last)
    def _():
        os_ref[...] = _ffn(xs_ref[...], nf_ref[...], win_ref, wout_ref,
                           act_ref.at[pl.ds(0, xs_ref.shape[0])])


def _ffn_call(xp2d, xs2d, nf, win, wout, layer, tile, cast_weights=()):
    n, d = xp2d.shape
    n_tiles = n // tile
    assert xs2d.shape[0] <= tile
    grid = (n_tiles + 1,)
    casts = _Casts(cast_weights, grid)
    prompt_spec = pl.BlockSpec((tile, d), lambda i: (jnp.minimum(i, n_tiles - 1), 0))
    return pl.pallas_call(
        casts.wrap(_ffn_kernel, 5, 2),
        grid=grid,
        in_specs=[prompt_spec, _const_spec(xs2d.shape), _layer_spec(nf.shape, layer),
                  _const_spec(win.shape), _const_spec(wout.shape)] + casts.in_specs,
        out_specs=[prompt_spec, pl.BlockSpec(xs2d.shape, lambda i: (0, 0))] + casts.out_specs,
        out_shape=[jax.ShapeDtypeStruct((n, d), F32),
                   jax.ShapeDtypeStruct(xs2d.shape, F32)] + casts.out_shape,
        scratch_shapes=[pltpu.VMEM((tile, wout.shape[0]), BF16)],
        compiler_params=_params(1),
        name="ffn",
    )(xp2d, xs2d, nf, win, wout, *casts.operands)


def _ret_head_proj(h, win_ref, hd, dk, dv):
    qk = RET_HEADS * dk
    vd = RET_HEADS * dv
    q = _dot(h, win_ref[:, hd * dk:(hd + 1) * dk])
    k = _dot(h, win_ref[:, qk + hd * dk:qk + (hd + 1) * dk]) * (dk ** -0.5)
    v = _dot(h, win_ref[:, 2 * qk + hd * dv:2 * qk + (hd + 1) * dv])
    gate = _dot(h, win_ref[:, 2 * qk + vd + hd * dv:2 * qk + vd + (hd + 1) * dv])
    return q, k, v, gate


def _ret_prompt_kernel(x_ref, cosr_ref, sinr_ref, cosb_ref, sinb_ref, g_ref, win_ref, wout_ref,
                       xo_ref, s_ref, dmask_ref, qdec_ref, kdec_ref, og_ref):
    t = og_ref.shape[0]
    c = dmask_ref.shape[1]
    dk = s_ref.shape[2]
    j = pl.program_id(1)

    @pl.when((pl.program_id(0) == 0) & (j == 0))
    def _():
        row = lax.broadcasted_iota(jnp.int32, (c, dk), 0).astype(F32)
        col = lax.broadcasted_iota(jnp.int32, (dk, c), 1).astype(F32)
        for hd in range(RET_HEADS):
            lg = _log_gamma(hd)
            dmask_ref[hd] = _decay_mask(lg, c)
            qdec_ref[hd] = jnp.exp(lg * (row + 1.0))
            kdec_ref[hd] = jnp.exp(lg * (c - 1.0 - col))

    @pl.when(j == 0)
    def _():
        s_ref[...] = jnp.zeros(s_ref.shape, F32)

    for sub in range(x_ref.shape[1] // t):
        rows = slice(sub * t, (sub + 1) * t)
        xo_ref[0, rows] = _ret_prompt_tile(
            x_ref[0, rows], cosb_ref[sub], sinb_ref[sub], cosr_ref, sinr_ref, g_ref, win_ref,
            wout_ref, s_ref, dmask_ref, qdec_ref, kdec_ref, og_ref)


def _ret_prompt_tile(x, cosb, sinb, cosr_ref, sinr_ref, g_ref, win_ref, wout_ref,
                     s_ref, dmask_ref, qdec_ref, kdec_ref, og_ref):
    t = x.shape[0]
    c = dmask_ref.shape[1]
    dk, dv = s_ref.shape[2], s_ref.shape[3]
    h = _rms(x, g_ref[...]).astype(BF16)
    cosr, sinr = cosr_ref[...], sinr_ref[...]
    cos = cosb * cosr - sinb * sinr
    sin = sinb * cosr + cosb * sinr
    qk, vd = RET_HEADS * dk, RET_HEADS * dv
    n_chunks = t // c
    proj = [dict() for _ in range(RET_HEADS)]
    out = [x]

    def proj_steps(hd):
        def q_step():
            proj[hd]["q"] = _rotary(_dot(h, win_ref[:, hd * dk:(hd + 1) * dk]), cos, sin)

        def k_step():
            k = _dot(h, win_ref[:, qk + hd * dk:qk + (hd + 1) * dk]) * (dk ** -0.5)
            proj[hd]["k_t"] = _rotary(k, cos, sin).T

        def v_step():
            lo = 2 * qk + hd * dv
            proj[hd]["v"] = _dot(h, win_ref[:, lo:lo + dv]).astype(BF16)

        def gate_step():
            lo = 2 * qk + vd + hd * dv
            proj[hd]["gate"] = _dot(h, win_ref[:, lo:lo + dv])

        return [q_step, k_step, v_step, gate_step]

    def out_step(hd):
        def step():
            out[0] = out[0] + _dot(og_ref[:, hd * dv:(hd + 1) * dv],
                                   wout_ref[hd * dv:(hd + 1) * dv, :])
        return step

    def core(hd, fillers):
        lg = _log_gamma(hd)
        q, k_t, v, gate = (proj[hd][name] for name in ("q", "k_t", "v", "gate"))
        chunks = [slice(ci * c, (ci + 1) * c) for ci in range(n_chunks)]
        scores = [_dot(q[rows].astype(BF16), k_t[:, rows].astype(BF16)) for rows in chunks]
        fillers.pop(0)()
        s_cur = s_ref[0, hd]
        for rows, sc in zip(chunks, scores):
            vc = v[rows]
            q_dec = (q[rows] * qdec_ref[hd]).astype(BF16)
            k_dec_t = (k_t[:, rows] * kdec_ref[hd]).astype(BF16)
            lhs = jnp.concatenate([(sc * dmask_ref[hd]).astype(BF16), q_dec], axis=1)
            rhs = jnp.concatenate([vc, s_cur.astype(BF16)], axis=0)
            o = _dot(lhs, rhs)
            s_cur = math.exp(lg * c) * s_cur + _dot(k_dec_t, vc)
            if fillers:
                fillers.pop(0)()
            o = o * lax.rsqrt(jnp.mean(o * o, axis=-1, keepdims=True) + EPS)
            og_ref[rows, hd * dv:(hd + 1) * dv] = (_silu(gate[rows]) * o).astype(BF16)
        s_ref[0, hd] = s_cur
        for filler in fillers:
            filler()

    for step in proj_steps(0):
        step()
    for hd in range(RET_HEADS):
        if hd + 1 < RET_HEADS:
            fillers = proj_steps(hd + 1)
        else:
            fillers = [out_step(i) for i in range(hd)]
        core(hd, fillers)
    out_step(RET_HEADS - 1)()
    return out[0]


def _ret_prompt_call(x, rope, s_shape, g, win, wout, mix_layer, cast_weights=()):
    b, l, d = x.shape
    nh, dk, dv = s_shape
    t, c = PROMPT_TILE, RET_CHUNK
    cosr, sinr, cosb, sinb = rope
    half = cosr.shape[1]
    n_sub = RET_TILES_PER_STEP
    base_spec = pl.BlockSpec((n_sub, 1, half), lambda i, j: (j, 0, 0))
    grid = (b, l // (n_sub * t))
    casts = _Casts(cast_weights, grid)
    return pl.pallas_call(
        casts.wrap(_ret_prompt_kernel, 8, 2),
        grid=grid,
        in_specs=[pl.BlockSpec((1, n_sub * t, d), lambda i, j: (i, j, 0)),
                  _const_spec(cosr.shape), _const_spec(sinr.shape), base_spec, base_spec,
                  _layer_spec(g.shape, mix_layer), _const_spec(win.shape),
                  _const_spec(wout.shape)] + casts.in_specs,
        out_specs=[pl.BlockSpec((1, n_sub * t, d), lambda i, j: (i, j, 0)),
                   pl.BlockSpec((1, nh, dk, dv), lambda i, j: (i, 0, 0, 0))] + casts.out_specs,
        out_shape=[jax.ShapeDtypeStruct((b, l, d), F32),
                   jax.ShapeDtypeStruct((b, nh, dk, dv), F32)] + casts.out_shape,
        scratch_shapes=[pltpu.VMEM((nh, c, c), F32), pltpu.VMEM((nh, c, dk), F32),
                        pltpu.VMEM((nh, dk, c), F32), pltpu.VMEM((t, nh * dv), BF16)],
        compiler_params=_params(2),
        name="ret_prompt",
    )(x, cosr, sinr, cosb, sinb, g, win, wout, *casts.operands)


def _ret_sample_kernel(x_ref, cos_ref, sin_ref, s0_ref, g_ref, win_ref, wout_ref, *rest, seg):
    earlier = rest[:-7]
    xo_ref, s_ref, q_ref, k_ref, v_ref, gate_ref, og_ref = rest[-7:]
    if earlier:
        for layer, prev_ref in enumerate(earlier):
            s_ref[layer] = prev_ref[...]
        s_ref = s_ref.at[len(earlier)]
    dk, dv = s_ref.shape[2], s_ref.shape[3]
    s = pl.program_id(0)

    @pl.when(s == 0)
    def _():
        h = _rms(x_ref[...], g_ref[...]).astype(BF16)
        cos, sin = cos_ref[...], sin_ref[...]
        for hd in range(RET_HEADS):
            q, k, v, gate = _ret_head_proj(h, win_ref, hd, dk, dv)
            q_ref[:, hd * dk:(hd + 1) * dk] = _rotary(q, cos, sin)
            k_ref[:, hd * dk:(hd + 1) * dk] = _rotary(k, cos, sin)
            v_ref[:, hd * dv:(hd + 1) * dv] = v
            gate_ref[:, hd * dv:(hd + 1) * dv] = gate

    rows = pl.ds(pl.multiple_of(s * seg, seg), seg)
    idx = lax.broadcasted_iota(jnp.int32, (seg, 1), 0).astype(F32)
    for hd in range(RET_HEADS):
        lg = _log_gamma(hd)
        q = q_ref[rows, hd * dk:(hd + 1) * dk]
        k = k_ref[rows, hd * dk:(hd + 1) * dk]
        v = v_ref[rows, hd * dv:(hd + 1) * dv].astype(BF16)
        gate = gate_ref[rows, hd * dv:(hd + 1) * dv]
        q_dec = (q * jnp.exp(lg * (idx + 1.0))).astype(BF16)
        k_dec = (k * jnp.exp(lg * (seg - 1.0 - idx))).astype(BF16)
        scores = _dot_nt(q.astype(BF16), k.astype(BF16)) * _decay_mask(lg, seg)
        s_old = s0_ref[0, hd]
        o = _dot(scores.astype(BF16), v) + _dot(q_dec, s_old.astype(BF16))
        s_ref[0, hd] = math.exp(lg * seg) * s_old + _dot_tn(k_dec, v)
        o = o * lax.rsqrt(jnp.mean(o * o, axis=-1, keepdims=True) + EPS)
        og_ref[rows, hd * dv:(hd + 1) * dv] = _silu(gate) * o

    @pl.when(s == pl.num_programs(0) - 1)
    def _():
        xo_ref[...] = x_ref[...] + _dot(og_ref[...].astype(BF16), wout_ref[...])


def _ret_sample_call(x, cos, sin, state, g, win, wout, mix_layer, ret_layer, earlier):
    ns, seg, d = x.shape
    _, _, nh, dk, dv = state.shape
    n = ns * seg
    stream_spec = pl.BlockSpec((1, nh, dk, dv), lambda i: (i, 0, 0, 0))
    if earlier:
        n_out = len(earlier) + 1
        out_state = jax.ShapeDtypeStruct((n_out, ns, nh, dk, dv), F32)
        out_state_spec = pl.BlockSpec((n_out, 1, nh, dk, dv), lambda i: (0, i, 0, 0, 0))
    else:
        out_state = jax.ShapeDtypeStruct((ns, nh, dk, dv), F32)
        out_state_spec = stream_spec
    out, s_new = pl.pallas_call(
        functools.partial(_ret_sample_kernel, seg=seg),
        grid=(ns,),
        in_specs=[_const_spec((n, d)), _const_spec(cos.shape), _const_spec(sin.shape),
                  pl.BlockSpec((None, 1, nh, dk, dv), lambda i: (ret_layer, i, 0, 0, 0)),
                  _layer_spec(g.shape, mix_layer), _const_spec(win.shape),
                  _const_spec(wout.shape)] + [stream_spec] * len(earlier),
        out_specs=[pl.BlockSpec((n, d), lambda i: (0, 0)), out_state_spec],
        out_shape=[jax.ShapeDtypeStruct((n, d), F32), out_state],
        scratch_shapes=[pltpu.VMEM((n, nh * dk), F32), pltpu.VMEM((n, nh * dk), F32),
                        pltpu.VMEM((n, nh * dv), F32), pltpu.VMEM((n, nh * dv), F32),
                        pltpu.VMEM((n, nh * dv), F32)],
        compiler_params=_params(1),
        name="ret_sample",
    )(x.reshape(n, d), cos, sin, state, g, win, wout, *earlier)
    return out.reshape(ns, seg, d), s_new


def _pool_mixer(x_ref, nm_ref, wp_ref, ps_ref, xp_ref, bo_ref, pos_start):
    nseg, lt, d = x_ref.shape
    gw = d // len(POOL_WINDOWS)
    x = x_ref[...]
    h = _rms(x, nm_ref[...])
    xp_ref[:, POOL_HIST:, :] = h
    pos = pos_start + lax.broadcasted_iota(jnp.int32, (1, lt, 1), 1)
    ys = []
    for gi, w in enumerate(POOL_WINDOWS):
        cols = slice(gi * gw, (gi + 1) * gw)
        acc = xp_ref[:, :, cols]
        span = 1
        while span < w:
            acc = acc + pltpu.roll(acc, span, axis=1)
            span *= 2
        cnt = jnp.minimum(pos + 1, w).astype(F32)
        dg = (acc[:, POOL_HIST:, :] / cnt - h[:, :, cols]).astype(BF16).reshape(nseg * lt, gw)
        ys.append(_dot(dg, wp_ref[gi]))
    y = jnp.concatenate(ys, axis=-1) * ps_ref[...]
    tail = xp_ref[:, lt:lt + POOL_HIST, :]
    bo_ref[...] = tail
    xp_ref[:, 0:POOL_HIST, :] = tail
    return x.reshape(nseg * lt, d) + y


def _pool_ffn_kernel(xp_ref, bufp_ref, xs_ref, bufs_ref, nm_ref, wp_ref, ps_ref, nf_ref, win_ref,
                     wout_ref, nfin_ref, op_ref, bop_ref, os_ref, bos_ref, hist_p_ref, hist_s_ref,
                     act_ref, *, seq_tiles, sample_pos0, final):
    s = pl.program_id(0)
    last = pl.num_programs(0) - 1

    def tile(x_ref, hist_ref, bo_ref, o_ref, act, pos_start):
        nseg, lt, d = x_ref.shape
        x1 = _pool_mixer(x_ref, nm_ref, wp_ref, ps_ref, hist_ref, bo_ref, pos_start)
        x2 = _ffn(x1, nf_ref[...], win_ref, wout_ref, act)
        if final:
            x2 = _rms(x2, nfin_ref[...])
        o_ref[...] = x2.reshape(nseg, lt, d)

    @pl.when(s < last)
    def _():
        j = s % seq_tiles

        @pl.when(j == 0)
        def _():
            hist_p_ref[:, 0:POOL_HIST, :] = bufp_ref[...]

        tile(xp_ref, hist_p_ref, bop_ref, op_ref, act_ref, j * xp_ref.shape[1])

    @pl.when(s == last)
    def _():
        hist_s_ref[:, 0:POOL_HIST, :] = bufs_ref[...]
        rows = xs_ref.shape[0] * xs_ref.shape[1]
        tile(xs_ref, hist_s_ref, bos_ref, os_ref, act_ref.at[pl.ds(0, rows)], sample_pos0)


def _pool_ffn_call(xp, bufp, xs, bufs, nm, wp, ps, nf, win, wout, nfin, *, mix_layer, pool_layer,
                   tile, sample_pos0, final, cast_weights=()):
    bp, lp, d = xp.shape
    bs, ls, _ = xs.shape
    seq_tiles = lp // tile
    n_tiles = bp * seq_tiles
    assert bs * ls <= tile
    grid = (n_tiles + 1,)
    casts = _Casts(cast_weights, grid)

    def prompt_tile(s):
        t = jnp.minimum(s, n_tiles - 1)
        return t // seq_tiles, t % seq_tiles

    x_spec = pl.BlockSpec((1, tile, d), lambda s: (*prompt_tile(s), 0))
    hist_spec = pl.BlockSpec((1, POOL_HIST, d), lambda s: (prompt_tile(s)[0], 0, 0))
    return pl.pallas_call(
        casts.wrap(functools.partial(_pool_ffn_kernel, seq_tiles=seq_tiles,
                                     sample_pos0=sample_pos0, final=final), 11, 4),
        grid=grid,
        in_specs=[x_spec, hist_spec, _const_spec(xs.shape), _const_spec(bufs.shape),
                  _layer_spec(nm.shape, mix_layer), _layer_spec(wp.shape, pool_layer),
                  _layer_spec(ps.shape, pool_layer), _layer_spec(nf.shape, mix_layer),
                  _const_spec(win.shape), _const_spec(wout.shape),
                  _const_spec(nfin.shape)] + casts.in_specs,
        out_specs=[x_spec, hist_spec, pl.BlockSpec(xs.shape, lambda s: (0, 0, 0)),
                   pl.BlockSpec(bufs.shape, lambda s: (0, 0, 0))] + casts.out_specs,
        out_shape=[jax.ShapeDtypeStruct(xp.shape, F32),
                   jax.ShapeDtypeStruct((bp, POOL_HIST, d), F32),
                   jax.ShapeDtypeStruct(xs.shape, F32),
                   jax.ShapeDtypeStruct(bufs.shape, F32)] + casts.out_shape,
        scratch_shapes=[pltpu.VMEM((1, POOL_HIST + tile, d), F32),
                        pltpu.VMEM((bs, POOL_HIST + ls, d), F32),
                        pltpu.VMEM((tile, wout.shape[0]), BF16)],
        compiler_params=_params(1),
        name="pool_ffn",
    )(xp, bufp, xs, bufs, nm, wp, ps, nf, win, wout, nfin, *casts.operands)


def _rope_angles(pos, half):
    inv = ROPE_BASE ** (-np.arange(half, dtype=np.float64) / half)
    return np.asarray(pos, np.float64)[:, None] * inv[None, :]


def _cos_sin(ang):
    return jnp.asarray(np.cos(ang), F32), jnp.asarray(np.sin(ang), F32)


def kernel(x_prompt, x_sample, state_ret, state_pool, norm_mix, w_ret_in, w_ret_out, w_pool,
           pool_scale, norm_ffn, w_ffn_in, w_ffn_out, norm_final):
    bp, lp, d = x_prompt.shape
    bs, ls, _ = x_sample.shape
    depth = norm_mix.shape[0]
    n_ret, _, nh, dk, dv = state_ret.shape
    assert nh == RET_HEADS and state_pool.shape[2] == POOL_HIST - 1
    assert lp % (PROMPT_TILE * RET_TILES_PER_STEP) == 0 and PROMPT_TILE % RET_CHUNK == 0
    assert ls >= POOL_HIST
    assert (bp * lp) % FFN_TILE == 0
    assert depth % 2 == 0

    ret_w = (w_ret_in[0].astype(BF16), w_ret_out[0].astype(BF16))
    w_pool_b = w_pool.astype(BF16)
    norm_mix3 = norm_mix[:, None, :]
    norm_ffn3 = norm_ffn[:, None, :]
    pool_scale3 = pool_scale[:, None, :]
    nfin = norm_final[None, :]

    half = dk // 2
    cosr, sinr = _cos_sin(_rope_angles(np.arange(PROMPT_TILE), half))
    cosb, sinb = _cos_sin(_rope_angles(np.arange(0, lp, PROMPT_TILE), half)[:, None, :])
    rope_p = (cosr, sinr, cosb, sinb)
    cos_s, sin_s = _cos_sin(np.tile(_rope_angles(PAST_LEN + np.arange(ls), half), (bs, 1)))

    pool0_p = jnp.zeros((bp, POOL_HIST, d), state_pool.dtype)
    pool0_s = jnp.pad(state_pool, ((0, 0), (0, 0), (1, 0), (0, 0)))

    xp, xs = x_prompt, x_sample
    ret_p, ret_s, pool_p, pool_s = [], [], [], []
    def ffn_weights(layer):
        return [(w_ffn_in, layer), (w_ffn_out, layer)]

    for i in range(depth):
        jm = i // 2
        if i % 2 == 0:
            xp, sp, *ffn_w = _ret_prompt_call(xp, rope_p, (nh, dk, dv), norm_mix3, *ret_w, i,
                                              cast_weights=ffn_weights(i))
            earlier = list(ret_s) if jm == n_ret - 1 else []
            xs, ss = _ret_sample_call(xs, cos_s, sin_s, state_ret, norm_mix3, *ret_w, i, jm,
                                      earlier)
            ret_s.append(ss)
            ret_p.append(sp)
            xp, xs, *ffn_w = _ffn_call(xp.reshape(bp * lp, d), xs.reshape(bs * ls, d), norm_ffn3,
                                       *ffn_w, i, FFN_TILE, cast_weights=ffn_weights(i + 1))
            xp, xs = xp.reshape(bp, lp, d), xs.reshape(bs, ls, d)
        else:
            next_ret = [(w_ret_in, jm + 1), (w_ret_out, jm + 1)] if i + 1 < depth else []
            xp, bpn, xs, bsn, *ret_w = _pool_ffn_call(
                xp, pool0_p, xs, pool0_s[jm], norm_mix3, w_pool_b, pool_scale3, norm_ffn3, *ffn_w,
                nfin, mix_layer=i, pool_layer=jm, tile=PROMPT_TILE, sample_pos0=PAST_LEN,
                final=i == depth - 1, cast_weights=next_ret)
            pool_p.append(bpn[:, 1:, :])
            pool_s.append(bsn[:, 1:, :])
    ret_s_stack = ret_s[-1] if n_ret > 1 else ret_s[0][None]
    return (xp, xs, jnp.stack(ret_p), ret_s_stack, jnp.stack(pool_p), jnp.stack(pool_s))
```

```python
import functools
import math

import numpy as np

import jax
import jax.numpy as jnp
from jax import lax
from jax.experimental import pallas as pl
from jax.experimental.pallas import tpu as pltpu

F32 = jnp.float32
BF16 = jnp.bfloat16

EPS = 1e-6
ROPE_BASE = 10000.0
PAST_LEN = 1024
RET_HEADS = 4
POOL_WINDOWS = (2, 4, 8, 16)
POOL_HIST = 16
PROMPT_TILE = 512
RET_CHUNK = 256
RET_TILES_PER_STEP = 2
FFN_TILE = 1024
FFN_CHUNK = 256
BF16_ROWS = 16
VMEM_LIMIT_BYTES = 56 * 1024 * 1024


def _log_gamma(head):
    return math.log1p(-(2.0 ** (-5.0 - head)))


def _rms(x, g):
    return (x * lax.rsqrt(jnp.mean(x * x, axis=-1, keepdims=True) + EPS)) * g


def _dot(a, b):
    return jnp.dot(a, b, preferred_element_type=F32)


def _dot_nt(a, b):
    return lax.dot_general(a, b, (((1,), (1,)), ((), ())), preferred_element_type=F32)


def _dot_tn(a, b):
    return lax.dot_general(a, b, (((0,), (0,)), ((), ())), preferred_element_type=F32)


def _rotary(t, cos, sin):
    half = t.shape[-1] // 2
    t1, t2 = t[:, :half], t[:, half:]
    return jnp.concatenate([t1 * cos - t2 * sin, t1 * sin + t2 * cos], axis=-1)


def _silu(t):
    return t * jax.nn.sigmoid(t)


def _decay_mask(lg, n):
    r = lax.broadcasted_iota(jnp.int32, (n, n), 0)
    c = lax.broadcasted_iota(jnp.int32, (n, n), 1)
    diff = (r - c).astype(F32)
    return jnp.where(diff >= 0, jnp.exp(lg * jnp.maximum(diff, 0.0)), 0.0)


def _ffn(x, nf, win_ref, wout_ref, act_ref):
    d_ff = wout_ref.shape[0]
    h = _rms(x, nf).astype(BF16)
    for lo in range(0, d_ff, FFN_CHUNK):
        up = _dot(h, win_ref[:, d_ff + lo:d_ff + lo + FFN_CHUNK])
        act_ref[:, lo:lo + FFN_CHUNK] = (
            _silu(_dot(h, win_ref[:, lo:lo + FFN_CHUNK])) * up).astype(BF16)
    return x + _dot(act_ref[...], wout_ref[...])


def _const_spec(shape):
    nd = len(shape)
    return pl.BlockSpec(shape, lambda *_: (0,) * nd, pipeline_mode=pl.Buffered(1))


def _layer_spec(stacked_shape, layer):
    nd = len(stacked_shape)
    return pl.BlockSpec((None,) + tuple(stacked_shape[1:]),
                        lambda *_: (layer,) + (0,) * (nd - 1), pipeline_mode=pl.Buffered(1))


def _params(n_axes):
    return pltpu.CompilerParams(dimension_semantics=("arbitrary",) * n_axes,
                                vmem_limit_bytes=VMEM_LIMIT_BYTES)


class _Casts:
    def __init__(self, weights, grid):
        self.grid = grid
        self.in_specs, self.out_specs, self.out_shape, self.operands, self.n_blocks = [], [], [], [], []
        n_steps = math.prod(grid)
        for w, layer in weights:
            _, rows, cols = w.shape
            blk = next(b for b in range(BF16_ROWS, rows + 1, BF16_ROWS)
                       if rows % b == 0 and rows // b <= n_steps)
            n_blk = rows // blk
            self.in_specs.append(pl.BlockSpec(
                (None, blk, cols),
                lambda *g, n_blk=n_blk, layer=layer: (layer, jnp.minimum(self.step(*g), n_blk - 1), 0)))
            self.out_specs.append(pl.BlockSpec(
                (blk, cols), lambda *g, n_blk=n_blk: (jnp.minimum(self.step(*g), n_blk - 1), 0)))
            self.out_shape.append(jax.ShapeDtypeStruct((rows, cols), BF16))
            self.operands.append(w)
            self.n_blocks.append(n_blk)

    def __len__(self):
        return len(self.operands)

    def step(self, *g):
        flat = g[0]
        for idx, extent in zip(g[1:], self.grid[1:]):
            flat = flat * extent + idx
        return flat

    def wrap(self, body, n_in, n_out):
        n = len(self)

        def kernel(*refs):
            ins, srcs = refs[:n_in], refs[n_in:n_in + n]
            outs = refs[n_in + n:n_in + n + n_out]
            dsts = refs[n_in + n + n_out:n_in + 2 * n + n_out]
            body(*ins, *outs, *refs[n_in + 2 * n + n_out:])
            step = self.step(*(pl.program_id(a) for a in range(len(self.grid))))
            for src, dst, n_blk in zip(srcs, dsts, self.n_blocks):
                @pl.when(step < n_blk)
                def _():
                    dst[...] = src[...].astype(BF16)

        return kernel


def _ffn_kernel(xp_ref, xs_ref, nf_ref, win_ref, wout_ref, op_ref, os_ref, act_ref):
    i = pl.program_id(0)
    last = pl.num_programs(0) - 1

    @pl.when(i < last)
    def _():
        op_ref[...] = _ffn(xp_ref[...], nf_ref[...], win_ref, wout_ref, act_ref)

    @pl.when(i == last)
    def _():
        os_ref[...] = _ffn(xs_ref[...], nf_ref[...], win_ref, wout_ref,
                           act_ref.at[pl.ds(0, xs_ref.shape[0])])


def _ffn_call(xp2d, xs2d, nf, win, wout, layer, tile, cast_weights=()):
    n, d = xp2d.shape
    n_tiles = n // tile
    assert xs2d.shape[0] <= tile
    grid = (n_tiles + 1,)
    casts = _Casts(cast_weights, grid)
    prompt_spec = pl.BlockSpec((tile, d), lambda i: (jnp.minimum(i, n_tiles - 1), 0))
    return pl.pallas_call(
        casts.wrap(_ffn_kernel, 5, 2),
        grid=grid,
        in_specs=[prompt_spec, _const_spec(xs2d.shape), _layer_spec(nf.shape, layer),
                  _const_spec(win.shape), _const_spec(wout.shape)] + casts.in_specs,
        out_specs=[prompt_spec, pl.BlockSpec(xs2d.shape, lambda i: (0, 0))] + casts.out_specs,
        out_shape=[jax.ShapeDtypeStruct((n, d), F32),
                   jax.ShapeDtypeStruct(xs2d.shape, F32)] + casts.out_shape,
        scratch_shapes=[pltpu.VMEM((tile, wout.shape[0]), BF16)],
        compiler_params=_params(1),
        name="ffn",
    )(xp2d, xs2d, nf, win, wout, *casts.operands)


def _ret_head_proj(h, win_ref, hd, dk, dv):
    qk = RET_HEADS * dk
    vd = RET_HEADS * dv
    q = _dot(h, win_ref[:, hd * dk:(hd + 1) * dk])
    k = _dot(h, win_ref[:, qk + hd * dk:qk + (hd + 1) * dk]) * (dk ** -0.5)
    v = _dot(h, win_ref[:, 2 * qk + hd * dv:2 * qk + (hd + 1) * dv])
    gate = _dot(h, win_ref[:, 2 * qk + vd + hd * dv:2 * qk + vd + (hd + 1) * dv])
    return q, k, v, gate


def _ret_prompt_kernel(x_ref, cosr_ref, sinr_ref, cosb_ref, sinb_ref, g_ref, win_ref, wout_ref,
                       xo_ref, s_ref, dmask_ref, qdec_ref, kdec_ref, og_ref):
    t = og_ref.shape[0]
    c = dmask_ref.shape[1]
    dk = s_ref.shape[2]
    j = pl.program_id(1)

    @pl.when((pl.program_id(0) == 0) & (j == 0))
    def _():
        row = lax.broadcasted_iota(jnp.int32, (c, dk), 0).astype(F32)
        col = lax.broadcasted_iota(jnp.int32, (dk, c), 1).astype(F32)
        for hd in range(RET_HEADS):
            lg = _log_gamma(hd)
            dmask_ref[hd] = _decay_mask(lg, c)
            qdec_ref[hd] = jnp.exp(lg * (row + 1.0))
            kdec_ref[hd] = jnp.exp(lg * (c - 1.0 - col))

    @pl.when(j == 0)
    def _():
        s_ref[...] = jnp.zeros(s_ref.shape, F32)

    for sub in range(x_ref.shape[1] // t):
        rows = slice(sub * t, (sub + 1) * t)
        xo_ref[0, rows] = _ret_prompt_tile(
            x_ref[0, rows], cosb_ref[sub], sinb_ref[sub], cosr_ref, sinr_ref, g_ref, win_ref,
            wout_ref, s_ref, dmask_ref, qdec_ref, kdec_ref, og_ref)


def _ret_prompt_tile(x, cosb, sinb, cosr_ref, sinr_ref, g_ref, win_ref, wout_ref,
                     s_ref, dmask_ref, qdec_ref, kdec_ref, og_ref):
    t = x.shape[0]
    c = dmask_ref.shape[1]
    dk, dv = s_ref.shape[2], s_ref.shape[3]
    h = _rms(x, g_ref[...]).astype(BF16)
    cosr, sinr = cosr_ref[...], sinr_ref[...]
    cos = cosb * cosr - sinb * sinr
    sin = sinb * cosr + cosb * sinr
    qk, vd = RET_HEADS * dk, RET_HEADS * dv
    n_chunks = t // c
    proj = [dict() for _ in range(RET_HEADS)]
    out = [x]

    def proj_steps(hd):
        def q_step():
            proj[hd]["q"] = _rotary(_dot(h, win_ref[:, hd * dk:(hd + 1) * dk]), cos, sin)

        def k_step():
            k = _dot(h, win_ref[:, qk + hd * dk:qk + (hd + 1) * dk]) * (dk ** -0.5)
            proj[hd]["k_t"] = _rotary(k, cos, sin).T

        def v_step():
            lo = 2 * qk + hd * dv
            proj[hd]["v"] = _dot(h, win_ref[:, lo:lo + dv]).astype(BF16)

        def gate_step():
            lo = 2 * qk + vd + hd * dv
            proj[hd]["gate"] = _dot(h, win_ref[:, lo:lo + dv])

        return [q_step, k_step, v_step, gate_step]

    def out_step(hd):
        def step():
            out[0] = out[0] + _dot(og_ref[:, hd * dv:(hd + 1) * dv],
                                   wout_ref[hd * dv:(hd + 1) * dv, :])
        return step

    def core(hd, fillers):
        lg = _log_gamma(hd)
        q, k_t, v, gate = (proj[hd][name] for name in ("q", "k_t", "v", "gate"))
        chunks = [slice(ci * c, (ci + 1) * c) for ci in range(n_chunks)]
        scores = [_dot(q[rows].astype(BF16), k_t[:, rows].astype(BF16)) for rows in chunks]
        fillers.pop(0)()
        s_cur = s_ref[0, hd]
        for rows, sc in zip(chunks, scores):
            vc = v[rows]
            q_dec = (q[rows] * qdec_ref[hd]).astype(BF16)
            k_dec_t = (k_t[:, rows] * kdec_ref[hd]).astype(BF16)
            lhs = jnp.concatenate([(sc * dmask_ref[hd]).astype(BF16), q_dec], axis=1)
            rhs = jnp.concatenate([vc, s_cur.astype(BF16)], axis=0)
            o = _dot(lhs, rhs)
            s_cur = math.exp(lg * c) * s_cur + _dot(k_dec_t, vc)
            if fillers:
                fillers.pop(0)()
            o = o * lax.rsqrt(jnp.mean(o * o, axis=-1, keepdims=True) + EPS)
            og_ref[rows, hd * dv:(hd + 1) * dv] = (_silu(gate[rows]) * o).astype(BF16)
        s_ref[0, hd] = s_cur
        for filler in fillers:
            filler()

    for step in proj_steps(0):
        step()
    for hd in range(RET_HEADS):
        if hd + 1 < RET_HEADS:
            fillers = proj_steps(hd + 1)
        else:
            fillers = [out_step(i) for i in range(hd)]
        core(hd, fillers)
    out_step(RET_HEADS - 1)()
    return out[0]


def _ret_prompt_call(x, rope, s_shape, g, win, wout, mix_layer, cast_weights=()):
    b, l, d = x.shape
    nh, dk, dv = s_shape
    t, c = PROMPT_TILE, RET_CHUNK
    cosr, sinr, cosb, sinb = rope
    half = cosr.shape[1]
    n_sub = RET_TILES_PER_STEP
    base_spec = pl.BlockSpec((n_sub, 1, half), lambda i, j: (j, 0, 0))
    grid = (b, l // (n_sub * t))
    casts = _Casts(cast_weights, grid)
    return pl.pallas_call(
        casts.wrap(_ret_prompt_kernel, 8, 2),
        grid=grid,
        in_specs=[pl.BlockSpec((1, n_sub * t, d), lambda i, j: (i, j, 0)),
                  _const_spec(cosr.shape), _const_spec(sinr.shape), base_spec, base_spec,
                  _layer_spec(g.shape, mix_layer), _const_spec(win.shape),
                  _const_spec(wout.shape)] + casts.in_specs,
        out_specs=[pl.BlockSpec((1, n_sub * t, d), lambda i, j: (i, j, 0)),
                   pl.BlockSpec((1, nh, dk, dv), lambda i, j: (i, 0, 0, 0))] + casts.out_specs,
        out_shape=[jax.ShapeDtypeStruct((b, l, d), F32),
                   jax.ShapeDtypeStruct((b, nh, dk, dv), F32)] + casts.out_shape,
        scratch_shapes=[pltpu.VMEM((nh, c, c), F32), pltpu.VMEM((nh, c, dk), F32),
                        pltpu.VMEM((nh, dk, c), F32), pltpu.VMEM((t, nh * dv), BF16)],
        compiler_params=_params(2),
        name="ret_prompt",
    )(x, cosr, sinr, cosb, sinb, g, win, wout, *casts.operands)


def _own_state_slab(s_ref, layer):
    if s_ref.ndim == 4:
        return s_ref
    for other in range(s_ref.shape[0]):
        if other != layer:
            s_ref[other] = jnp.zeros(s_ref.shape[1:], F32)
    return s_ref.at[layer]


def _ret_sample_kernel(x_ref, cos_ref, sin_ref, s0_ref, g_ref, win_ref, wout_ref, *rest, seg,
                       layer):
    xo_ref, s_ref, q_ref, k_ref, v_ref, gate_ref, og_ref = rest[-7:]
    s_ref = _own_state_slab(s_ref, layer)
    dk, dv = s_ref.shape[2], s_ref.shape[3]
    s = pl.program_id(0)

    @pl.when(s == 0)
    def _():
        h = _rms(x_ref[...], g_ref[...]).astype(BF16)
        cos, sin = cos_ref[...], sin_ref[...]
        for hd in range(RET_HEADS):
            q, k, v, gate = _ret_head_proj(h, win_ref, hd, dk, dv)
            q_ref[:, hd * dk:(hd + 1) * dk] = _rotary(q, cos, sin)
            k_ref[:, hd * dk:(hd + 1) * dk] = _rotary(k, cos, sin)
            v_ref[:, hd * dv:(hd + 1) * dv] = v
            gate_ref[:, hd * dv:(hd + 1) * dv] = gate

    rows = pl.ds(pl.multiple_of(s * seg, seg), seg)
    idx = lax.broadcasted_iota(jnp.int32, (seg, 1), 0).astype(F32)
    for hd in range(RET_HEADS):
        lg = _log_gamma(hd)
        q = q_ref[rows, hd * dk:(hd + 1) * dk]
        k = k_ref[rows, hd * dk:(hd + 1) * dk]
        v = v_ref[rows, hd * dv:(hd + 1) * dv].astype(BF16)
        gate = gate_ref[rows, hd * dv:(hd + 1) * dv]
        q_dec = (q * jnp.exp(lg * (idx + 1.0))).astype(BF16)
        k_dec = (k * jnp.exp(lg * (seg - 1.0 - idx))).astype(BF16)
        scores = _dot_nt(q.astype(BF16), k.astype(BF16)) * _decay_mask(lg, seg)
        s_old = s0_ref[0, hd]
        o = _dot(scores.astype(BF16), v) + _dot(q_dec, s_old.astype(BF16))
        s_ref[0, hd] = math.exp(lg * seg) * s_old + _dot_tn(k_dec, v)
        o = o * lax.rsqrt(jnp.mean(o * o, axis=-1, keepdims=True) + EPS)
        og_ref[rows, hd * dv:(hd + 1) * dv] = _silu(gate) * o

    @pl.when(s == pl.num_programs(0) - 1)
    def _():
        xo_ref[...] = x_ref[...] + _dot(og_ref[...].astype(BF16), wout_ref[...])


def _ret_sample_call(x, cos, sin, state, g, win, wout, mix_layer, ret_layer, stack):
    ns, seg, d = x.shape
    n_ret, _, nh, dk, dv = state.shape
    n = ns * seg
    if stack is None:
        out_spec = pl.BlockSpec((n_ret, 1, nh, dk, dv), lambda i: (0, i, 0, 0, 0))
        extra_specs, extra = [], []
    else:
        out_spec = pl.BlockSpec((None, 1, nh, dk, dv), lambda i: (ret_layer, i, 0, 0, 0))
        extra_specs, extra = [pl.BlockSpec(memory_space=pl.ANY)], [stack]
    out, stack = pl.pallas_call(
        functools.partial(_ret_sample_kernel, seg=seg, layer=ret_layer),
        grid=(ns,),
        in_specs=[_const_spec((n, d)), _const_spec(cos.shape), _const_spec(sin.shape),
                  pl.BlockSpec((None, 1, nh, dk, dv), lambda i: (ret_layer, i, 0, 0, 0)),
                  _layer_spec(g.shape, mix_layer), _const_spec(win.shape),
                  _const_spec(wout.shape)] + extra_specs,
        out_specs=[pl.BlockSpec((n, d), lambda i: (0, 0)), out_spec],
        out_shape=[jax.ShapeDtypeStruct((n, d), F32), jax.ShapeDtypeStruct(state.shape, F32)],
        input_output_aliases={7: 1} if extra else {},
        scratch_shapes=[pltpu.VMEM((n, nh * dk), F32), pltpu.VMEM((n, nh * dk), F32),
                        pltpu.VMEM((n, nh * dv), F32), pltpu.VMEM((n, nh * dv), F32),
                        pltpu.VMEM((n, nh * dv), F32)],
        compiler_params=_params(1),
        name="ret_sample",
    )(x.reshape(n, d), cos, sin, state, g, win, wout, *extra)
    return out.reshape(ns, seg, d), stack


def _pool_mixer(x_ref, nm_ref, wp_ref, ps_ref, xp_ref, bo_ref, pos_start):
    nseg, lt, d = x_ref.shape
    gw = d // len(POOL_WINDOWS)
    x = x_ref[...]
    h = _rms(x, nm_ref[...])
    xp_ref[:, POOL_HIST:, :] = h
    pos = pos_start + lax.broadcasted_iota(jnp.int32, (1, lt, 1), 1)
    ys = []
    for gi, w in enumerate(POOL_WINDOWS):
        cols = slice(gi * gw, (gi + 1) * gw)
        acc = xp_ref[:, :, cols]
        span = 1
        while span < w:
            acc = acc + pltpu.roll(acc, span, axis=1)
            span *= 2
        cnt = jnp.minimum(pos + 1, w).astype(F32)
        dg = (acc[:, POOL_HIST:, :] / cnt - h[:, :, cols]).astype(BF16).reshape(nseg * lt, gw)
        ys.append(_dot(dg, wp_ref[gi]))
    y = jnp.concatenate(ys, axis=-1) * ps_ref[...]
    tail = xp_ref[:, lt:lt + POOL_HIST, :]
    bo_ref[...] = tail
    xp_ref[:, 0:POOL_HIST, :] = tail
    return x.reshape(nseg * lt, d) + y


def _pool_ffn_kernel(xp_ref, bufp_ref, xs_ref, bufs_ref, nm_ref, wp_ref, ps_ref, nf_ref, win_ref,
                     wout_ref, nfin_ref, op_ref, bop_ref, os_ref, bos_ref, hist_p_ref, hist_s_ref,
                     act_ref, *, seq_tiles, sample_pos0, final):
    s = pl.program_id(0)
    last = pl.num_programs(0) - 1

    def tile(x_ref, hist_ref, bo_ref, o_ref, act, pos_start):
        nseg, lt, d = x_ref.shape
        x1 = _pool_mixer(x_ref, nm_ref, wp_ref, ps_ref, hist_ref, bo_ref, pos_start)
        x2 = _ffn(x1, nf_ref[...], win_ref, wout_ref, act)
        if final:
            x2 = _rms(x2, nfin_ref[...])
        o_ref[...] = x2.reshape(nseg, lt, d)

    @pl.when(s < last)
    def _():
        j = s % seq_tiles

        @pl.when(j == 0)
        def _():
            hist_p_ref[:, 0:POOL_HIST, :] = bufp_ref[...]

        tile(xp_ref, hist_p_ref, bop_ref, op_ref, act_ref, j * xp_ref.shape[1])

    @pl.when(s == last)
    def _():
        hist_s_ref[:, 0:POOL_HIST, :] = bufs_ref[...]
        rows = xs_ref.shape[0] * xs_ref.shape[1]
        tile(xs_ref, hist_s_ref, bos_ref, os_ref, act_ref.at[pl.ds(0, rows)], sample_pos0)


def _pool_ffn_call(xp, bufp, xs, bufs, nm, wp, ps, nf, win, wout, nfin, *, mix_layer, pool_layer,
                   tile, sample_pos0, final, cast_weights=()):
    bp, lp, d = xp.shape
    bs, ls, _ = xs.shape
    seq_tiles = lp // tile
    n_tiles = bp * seq_tiles
    assert bs * ls <= tile
    grid = (n_tiles + 1,)
    casts = _Casts(cast_weights, grid)

    def prompt_tile(s):
        t = jnp.minimum(s, n_tiles - 1)
        return t // seq_tiles, t % seq_tiles

    x_spec = pl.BlockSpec((1, tile, d), lambda s: (*prompt_tile(s), 0))
    hist_spec = pl.BlockSpec((1, POOL_HIST, d), lambda s: (prompt_tile(s)[0], 0, 0))
    return pl.pallas_call(
        casts.wrap(functools.partial(_pool_ffn_kernel, seq_tiles=seq_tiles,
                                     sample_pos0=sample_pos0, final=final), 11, 4),
        grid=grid,
        in_specs=[x_spec, hist_spec, _const_spec(xs.shape), _const_spec(bufs.shape),
                  _layer_spec(nm.shape, mix_layer), _layer_spec(wp.shape, pool_layer),
                  _layer_spec(ps.shape, pool_layer), _layer_spec(nf.shape, mix_layer),
                  _const_spec(win.shape), _const_spec(wout.shape),
                  _const_spec(nfin.shape)] + casts.in_specs,
        out_specs=[x_spec, hist_spec, pl.BlockSpec(xs.shape, lambda s: (0, 0, 0)),
                   pl.BlockSpec(bufs.shape, lambda s: (0, 0, 0))] + casts.out_specs,
        out_shape=[jax.ShapeDtypeStruct(xp.shape, F32),
                   jax.ShapeDtypeStruct((bp, POOL_HIST, d), F32),
                   jax.ShapeDtypeStruct(xs.shape, F32),
                   jax.ShapeDtypeStruct(bufs.shape, F32)] + casts.out_shape,
        scratch_shapes=[pltpu.VMEM((1, POOL_HIST + tile, d), F32),
                        pltpu.VMEM((bs, POOL_HIST + ls, d), F32),
                        pltpu.VMEM((tile, wout.shape[0]), BF16)],
        compiler_params=_params(1),
        name="pool_ffn",
    )(xp, bufp, xs, bufs, nm, wp, ps, nf, win, wout, nfin, *casts.operands)


def _rope_angles(pos, half):
    inv = ROPE_BASE ** (-np.arange(half, dtype=np.float64) / half)
    return np.asarray(pos, np.float64)[:, None] * inv[None, :]


def _cos_sin(ang):
    return jnp.asarray(np.cos(ang), F32), jnp.asarray(np.sin(ang), F32)


def kernel(x_prompt, x_sample, state_ret, state_pool, norm_mix, w_ret_in, w_ret_out, w_pool,
           pool_scale, norm_ffn, w_ffn_in, w_ffn_out, norm_final):
    bp, lp, d = x_prompt.shape
    bs, ls, _ = x_sample.shape
    depth = norm_mix.shape[0]
    n_ret, _, nh, dk, dv = state_ret.shape
    assert nh == RET_HEADS and state_pool.shape[2] == POOL_HIST - 1
    assert lp % (PROMPT_TILE * RET_TILES_PER_STEP) == 0 and PROMPT_TILE % RET_CHUNK == 0
    assert ls >= POOL_HIST
    assert (bp * lp) % FFN_TILE == 0
    assert depth % 2 == 0

    ret_w = (w_ret_in[0].astype(BF16), w_ret_out[0].astype(BF16))
    w_pool_b = w_pool.astype(BF16)
    norm_mix3 = norm_mix[:, None, :]
    norm_ffn3 = norm_ffn[:, None, :]
    pool_scale3 = pool_scale[:, None, :]
    nfin = norm_final[None, :]

    half = dk // 2
    cosr, sinr = _cos_sin(_rope_angles(np.arange(PROMPT_TILE), half))
    cosb, sinb = _cos_sin(_rope_angles(np.arange(0, lp, PROMPT_TILE), half)[:, None, :])
    rope_p = (cosr, sinr, cosb, sinb)
    cos_s, sin_s = _cos_sin(np.tile(_rope_angles(PAST_LEN + np.arange(ls), half), (bs, 1)))

    pool0_p = jnp.zeros((bp, POOL_HIST, d), state_pool.dtype)
    pool0_s = jnp.pad(state_pool, ((0, 0), (0, 0), (1, 0), (0, 0)))

    xp, xs = x_prompt, x_sample
    ret_p, pool_p, pool_s = [], [], []
    ret_s = None

    def ffn_weights(layer):
        return [(w_ffn_in, layer), (w_ffn_out, layer)]

    for i in range(depth):
        jm = i // 2
        if i % 2 == 0:
            xp, sp, *ffn_w = _ret_prompt_call(xp, rope_p, (nh, dk, dv), norm_mix3, *ret_w, i,
                                              cast_weights=ffn_weights(i))
            ret_p.append(sp)
            xs, ret_s = _ret_sample_call(xs, cos_s, sin_s, state_ret, norm_mix3, *ret_w, i, jm,
                                         ret_s)
            xp, xs, *ffn_w = _ffn_call(xp.reshape(bp * lp, d), xs.reshape(bs * ls, d), norm_ffn3,
                                       *ffn_w, i, FFN_TILE, cast_weights=ffn_weights(i + 1))
            xp, xs = xp.reshape(bp, lp, d), xs.reshape(bs, ls, d)
        else:
            next_ret = [(w_ret_in, jm + 1), (w_ret_out, jm + 1)] if i + 1 < depth else []
            xp, bpn, xs, bsn, *ret_w = _pool_ffn_call(
                xp, pool0_p, xs, pool0_s[jm], norm_mix3, w_pool_b, pool_scale3, norm_ffn3, *ffn_w,
                nfin, mix_layer=i, pool_layer=jm, tile=PROMPT_TILE, sample_pos0=PAST_LEN,
                final=i == depth - 1, cast_weights=next_ret)
            pool_p.append(bpn[:, 1:, :])
            pool_s.append(bsn[:, 1:, :])
    return (xp, xs, jnp.stack(ret_p), ret_s, jnp.stack(pool_p), jnp.stack(pool_s))
```

```python
import functools
import math

import numpy as np

import jax
import jax.numpy as jnp
from jax import lax
from jax.experimental import pallas as pl
from jax.experimental.pallas import tpu as pltpu

F32 = jnp.float32
BF16 = jnp.bfloat16

EPS = 1e-6
ROPE_BASE = 10000.0
PAST_LEN = 1024
RET_HEADS = 4
POOL_WINDOWS = (2, 4, 8, 16)
POOL_HIST = 16
PROMPT_TILE = 512
RET_CHUNK = 256
RET_TILES_PER_STEP = 2
FFN_TILE = 1024
POOL_TILE = 1024
FFN_CHUNK = 256
BF16_ROWS = 16
VMEM_LIMIT_BYTES = 56 * 1024 * 1024


def _log_gamma(head):
    return math.log1p(-(2.0 ** (-5.0 - head)))


def _rms(x, g):
    return (x * lax.rsqrt(jnp.mean(x * x, axis=-1, keepdims=True) + EPS)) * g


def _dot(a, b):
    return jnp.dot(a, b, preferred_element_type=F32)


def _dot_nt(a, b):
    return lax.dot_general(a, b, (((1,), (1,)), ((), ())), preferred_element_type=F32)


def _dot_tn(a, b):
    return lax.dot_general(a, b, (((0,), (0,)), ((), ())), preferred_element_type=F32)


def _rotary(t, cos, sin):
    half = t.shape[-1] // 2
    t1, t2 = t[:, :half], t[:, half:]
    return jnp.concatenate([t1 * cos - t2 * sin, t1 * sin + t2 * cos], axis=-1)


def _silu(t):
    return t * jax.nn.sigmoid(t)


def _decay_mask(lg, n):
    r = lax.broadcasted_iota(jnp.int32, (n, n), 0)
    c = lax.broadcasted_iota(jnp.int32, (n, n), 1)
    diff = (r - c).astype(F32)
    return jnp.where(diff >= 0, jnp.exp(lg * jnp.maximum(diff, 0.0)), 0.0)


def _ffn(x, nf, win_ref, wout_ref, act_ref):
    d_ff = wout_ref.shape[0]
    h = _rms(x, nf).astype(BF16)
    for lo in range(0, d_ff, FFN_CHUNK):
        up = _dot(h, win_ref[:, d_ff + lo:d_ff + lo + FFN_CHUNK])
        act_ref[:, lo:lo + FFN_CHUNK] = (
            _silu(_dot(h, win_ref[:, lo:lo + FFN_CHUNK])) * up).astype(BF16)
    return x + _dot(act_ref[...], wout_ref[...])


def _const_spec(shape):
    nd = len(shape)
    return pl.BlockSpec(shape, lambda *_: (0,) * nd, pipeline_mode=pl.Buffered(1))


def _layer_spec(stacked_shape, layer):
    nd = len(stacked_shape)
    return pl.BlockSpec((None,) + tuple(stacked_shape[1:]),
                        lambda *_: (layer,) + (0,) * (nd - 1), pipeline_mode=pl.Buffered(1))


def _params(n_axes):
    return pltpu.CompilerParams(dimension_semantics=("arbitrary",) * n_axes,
                                vmem_limit_bytes=VMEM_LIMIT_BYTES)


class _Casts:
    def __init__(self, weights, grid):
        self.grid = grid
        self.in_specs, self.out_specs, self.out_shape, self.operands, self.n_blocks = [], [], [], [], []
        n_steps = math.prod(grid)
        for w, layer in weights:
            _, rows, cols = w.shape
            blk = next(b for b in range(BF16_ROWS, rows + 1, BF16_ROWS)
                       if rows % b == 0 and rows // b <= n_steps)
            n_blk = rows // blk
            self.in_specs.append(pl.BlockSpec(
                (None, blk, cols),
                lambda *g, n_blk=n_blk, layer=layer: (layer, jnp.minimum(self.step(*g), n_blk - 1), 0)))
            self.out_specs.append(pl.BlockSpec(
                (blk, cols), lambda *g, n_blk=n_blk: (jnp.minimum(self.step(*g), n_blk - 1), 0)))
            self.out_shape.append(jax.ShapeDtypeStruct((rows, cols), BF16))
            self.operands.append(w)
            self.n_blocks.append(n_blk)

    def __len__(self):
        return len(self.operands)

    def step(self, *g):
        flat = g[0]
        for idx, extent in zip(g[1:], self.grid[1:]):
            flat = flat * extent + idx
        return flat

    def wrap(self, body, n_in, n_out):
        n = len(self)

        def kernel(*refs):
            ins, srcs = refs[:n_in], refs[n_in:n_in + n]
            outs = refs[n_in + n:n_in + n + n_out]
            dsts = refs[n_in + n + n_out:n_in + 2 * n + n_out]
            body(*ins, *outs, *refs[n_in + 2 * n + n_out:])
            step = self.step(*(pl.program_id(a) for a in range(len(self.grid))))
            for src, dst, n_blk in zip(srcs, dsts, self.n_blocks):
                @pl.when(step < n_blk)
                def _():
                    dst[...] = src[...].astype(BF16)

        return kernel


def _ffn_kernel(xp_ref, xs_ref, nf_ref, win_ref, wout_ref, op_ref, os_ref, act_ref):
    i = pl.program_id(0)
    last = pl.num_programs(0) - 1

    @pl.when(i < last)
    def _():
        op_ref[...] = _ffn(xp_ref[...], nf_ref[...], win_ref, wout_ref, act_ref)

    @pl.when(i == last)
    def _():
        os_ref[...] = _ffn(xs_ref[...], nf_ref[...], win_ref, wout_ref,
                           act_ref.at[pl.ds(0, xs_ref.shape[0])])


def _ffn_call(xp2d, xs2d, nf, win, wout, layer, tile, cast_weights=()):
    n, d = xp2d.shape
    n_tiles = n // tile
    assert xs2d.shape[0] <= tile
    grid = (n_tiles + 1,)
    casts = _Casts(cast_weights, grid)
    prompt_spec = pl.BlockSpec((tile, d), lambda i: (jnp.minimum(i, n_tiles - 1), 0))
    return pl.pallas_call(
        casts.wrap(_ffn_kernel, 5, 2),
        grid=grid,
        in_specs=[prompt_spec, _const_spec(xs2d.shape), _layer_spec(nf.shape, layer),
                  _const_spec(win.shape), _const_spec(wout.shape)] + casts.in_specs,
        out_specs=[prompt_spec, pl.BlockSpec(xs2d.shape, lambda i: (0, 0))] + casts.out_specs,
        out_shape=[jax.ShapeDtypeStruct((n, d), F32),
                   jax.ShapeDtypeStruct(xs2d.shape, F32)] + casts.out_shape,
        scratch_shapes=[pltpu.VMEM((tile, wout.shape[0]), BF16)],
        compiler_params=_params(1),
        name="ffn",
    )(xp2d, xs2d, nf, win, wout, *casts.operands)


def _ret_head_proj(h, win_ref, hd, dk, dv):
    qk = RET_HEADS * dk
    vd = RET_HEADS * dv
    q = _dot(h, win_ref[:, hd * dk:(hd + 1) * dk])
    k = _dot(h, win_ref[:, qk + hd * dk:qk + (hd + 1) * dk]) * (dk ** -0.5)
    v = _dot(h, win_ref[:, 2 * qk + hd * dv:2 * qk + (hd + 1) * dv])
    gate = _dot(h, win_ref[:, 2 * qk + vd + hd * dv:2 * qk + vd + (hd + 1) * dv])
    return q, k, v, gate


def _ret_prompt_kernel(x_ref, cosr_ref, sinr_ref, cosb_ref, sinb_ref, g_ref, win_ref, wout_ref,
                       xo_ref, s_ref, dmask_ref, qdec_ref, kdec_ref, og_ref):
    t = og_ref.shape[0]
    c = dmask_ref.shape[1]
    dk = s_ref.shape[2]
    j = pl.program_id(1)

    @pl.when((pl.program_id(0) == 0) & (j == 0))
    def _():
        row = lax.broadcasted_iota(jnp.int32, (c, dk), 0).astype(F32)
        col = lax.broadcasted_iota(jnp.int32, (dk, c), 1).astype(F32)
        for hd in range(RET_HEADS):
            lg = _log_gamma(hd)
            dmask_ref[hd] = _decay_mask(lg, c)
            qdec_ref[hd] = jnp.exp(lg * (row + 1.0))
            kdec_ref[hd] = jnp.exp(lg * (c - 1.0 - col))

    @pl.when(j == 0)
    def _():
        s_ref[...] = jnp.zeros(s_ref.shape, F32)

    for sub in range(x_ref.shape[1] // t):
        rows = slice(sub * t, (sub + 1) * t)
        xo_ref[0, rows] = _ret_prompt_tile(
            x_ref[0, rows], cosb_ref[sub], sinb_ref[sub], cosr_ref, sinr_ref, g_ref, win_ref,
            wout_ref, s_ref, dmask_ref, qdec_ref, kdec_ref, og_ref)


def _ret_prompt_tile(x, cosb, sinb, cosr_ref, sinr_ref, g_ref, win_ref, wout_ref,
                     s_ref, dmask_ref, qdec_ref, kdec_ref, og_ref):
    t = x.shape[0]
    c = dmask_ref.shape[1]
    dk, dv = s_ref.shape[2], s_ref.shape[3]
    h = _rms(x, g_ref[...]).astype(BF16)
    cosr, sinr = cosr_ref[...], sinr_ref[...]
    cos = cosb * cosr - sinb * sinr
    sin = sinb * cosr + cosb * sinr
    qk, vd = RET_HEADS * dk, RET_HEADS * dv
    n_chunks = t // c
    proj = [dict() for _ in range(RET_HEADS)]
    out = [x]

    def proj_steps(hd):
        def q_step():
            proj[hd]["q"] = _rotary(_dot(h, win_ref[:, hd * dk:(hd + 1) * dk]), cos, sin)

        def k_step():
            k = _dot(h, win_ref[:, qk + hd * dk:qk + (hd + 1) * dk]) * (dk ** -0.5)
            proj[hd]["k_t"] = _rotary(k, cos, sin).T

        def v_step():
            lo = 2 * qk + hd * dv
            proj[hd]["v"] = _dot(h, win_ref[:, lo:lo + dv]).astype(BF16)

        def gate_step():
            lo = 2 * qk + vd + hd * dv
            proj[hd]["gate"] = _dot(h, win_ref[:, lo:lo + dv])

        return [q_step, k_step, v_step, gate_step]

    def out_step(hd):
        def step():
            out[0] = out[0] + _dot(og_ref[:, hd * dv:(hd + 1) * dv],
                                   wout_ref[hd * dv:(hd + 1) * dv, :])
        return step

    def core(hd, fillers):
        lg = _log_gamma(hd)
        q, k_t, v, gate = (proj[hd][name] for name in ("q", "k_t", "v", "gate"))
        chunks = [slice(ci * c, (ci + 1) * c) for ci in range(n_chunks)]
        scores = [_dot(q[rows].astype(BF16), k_t[:, rows].astype(BF16)) for rows in chunks]
        fillers.pop(0)()
        s_cur = s_ref[0, hd]
        for rows, sc in zip(chunks, scores):
            vc = v[rows]
            q_dec = (q[rows] * qdec_ref[hd]).astype(BF16)
            k_dec_t = (k_t[:, rows] * kdec_ref[hd]).astype(BF16)
            lhs = jnp.concatenate([(sc * dmask_ref[hd]).astype(BF16), q_dec], axis=1)
            rhs = jnp.concatenate([vc, s_cur.astype(BF16)], axis=0)
            o = _dot(lhs, rhs)
            s_cur = math.exp(lg * c) * s_cur + _dot(k_dec_t, vc)
            if fillers:
                fillers.pop(0)()
            o = o * lax.rsqrt(jnp.mean(o * o, axis=-1, keepdims=True) + EPS)
            og_ref[rows, hd * dv:(hd + 1) * dv] = (_silu(gate[rows]) * o).astype(BF16)
        s_ref[0, hd] = s_cur
        for filler in fillers:
            filler()

    for step in proj_steps(0):
        step()
    for hd in range(RET_HEADS):
        if hd + 1 < RET_HEADS:
            fillers = proj_steps(hd + 1)
        else:
            fillers = [out_step(i) for i in range(hd)]
        core(hd, fillers)
    out_step(RET_HEADS - 1)()
    return out[0]


def _ret_prompt_call(x, rope, s_shape, g, win, wout, mix_layer, cast_weights=()):
    b, l, d = x.shape
    nh, dk, dv = s_shape
    t, c = PROMPT_TILE, RET_CHUNK
    cosr, sinr, cosb, sinb = rope
    half = cosr.shape[1]
    n_sub = RET_TILES_PER_STEP
    base_spec = pl.BlockSpec((n_sub, 1, half), lambda i, j: (j, 0, 0))
    grid = (b, l // (n_sub * t))
    casts = _Casts(cast_weights, grid)
    return pl.pallas_call(
        casts.wrap(_ret_prompt_kernel, 8, 2),
        grid=grid,
        in_specs=[pl.BlockSpec((1, n_sub * t, d), lambda i, j: (i, j, 0)),
                  _const_spec(cosr.shape), _const_spec(sinr.shape), base_spec, base_spec,
                  _layer_spec(g.shape, mix_layer), _const_spec(win.shape),
                  _const_spec(wout.shape)] + casts.in_specs,
        out_specs=[pl.BlockSpec((1, n_sub * t, d), lambda i, j: (i, j, 0)),
                   pl.BlockSpec((1, nh, dk, dv), lambda i, j: (i, 0, 0, 0))] + casts.out_specs,
        out_shape=[jax.ShapeDtypeStruct((b, l, d), F32),
                   jax.ShapeDtypeStruct((b, nh, dk, dv), F32)] + casts.out_shape,
        scratch_shapes=[pltpu.VMEM((nh, c, c), F32), pltpu.VMEM((nh, c, dk), F32),
                        pltpu.VMEM((nh, dk, c), F32), pltpu.VMEM((t, nh * dv), BF16)],
        compiler_params=_params(2),
        name="ret_prompt",
    )(x, cosr, sinr, cosb, sinb, g, win, wout, *casts.operands)


def _own_state_slab(s_ref, layer):
    if s_ref.ndim == 4:
        return s_ref
    for other in range(s_ref.shape[0]):
        if other != layer:
            s_ref[other] = jnp.zeros(s_ref.shape[1:], F32)
    return s_ref.at[layer]


def _ret_sample_kernel(x_ref, cos_ref, sin_ref, s0_ref, g_ref, win_ref, wout_ref, *rest, seg,
                       layer):
    xo_ref, s_ref, q_ref, k_ref, v_ref, gate_ref, og_ref = rest[-7:]
    s_ref = _own_state_slab(s_ref, layer)
    dk, dv = s_ref.shape[2], s_ref.shape[3]
    s = pl.program_id(0)

    @pl.when(s == 0)
    def _():
        h = _rms(x_ref[...], g_ref[...]).astype(BF16)
        cos, sin = cos_ref[...], sin_ref[...]
        for hd in range(RET_HEADS):
            q, k, v, gate = _ret_head_proj(h, win_ref, hd, dk, dv)
            q_ref[:, hd * dk:(hd + 1) * dk] = _rotary(q, cos, sin)
            k_ref[:, hd * dk:(hd + 1) * dk] = _rotary(k, cos, sin)
            v_ref[:, hd * dv:(hd + 1) * dv] = v
            gate_ref[:, hd * dv:(hd + 1) * dv] = gate

    rows = pl.ds(pl.multiple_of(s * seg, seg), seg)
    idx = lax.broadcasted_iota(jnp.int32, (seg, 1), 0).astype(F32)
    for hd in range(RET_HEADS):
        lg = _log_gamma(hd)
        q = q_ref[rows, hd * dk:(hd + 1) * dk]
        k = k_ref[rows, hd * dk:(hd + 1) * dk]
        v = v_ref[rows, hd * dv:(hd + 1) * dv].astype(BF16)
        gate = gate_ref[rows, hd * dv:(hd + 1) * dv]
        q_dec = (q * jnp.exp(lg * (idx + 1.0))).astype(BF16)
        k_dec = (k * jnp.exp(lg * (seg - 1.0 - idx))).astype(BF16)
        scores = _dot_nt(q.astype(BF16), k.astype(BF16)) * _decay_mask(lg, seg)
        s_old = s0_ref[0, hd]
        o = _dot(scores.astype(BF16), v) + _dot(q_dec, s_old.astype(BF16))
        s_ref[0, hd] = math.exp(lg * seg) * s_old + _dot_tn(k_dec, v)
        o = o * lax.rsqrt(jnp.mean(o * o, axis=-1, keepdims=True) + EPS)
        og_ref[rows, hd * dv:(hd + 1) * dv] = _silu(gate) * o

    @pl.when(s == pl.num_programs(0) - 1)
    def _():
        xo_ref[...] = x_ref[...] + _dot(og_ref[...].astype(BF16), wout_ref[...])


def _ret_sample_call(x, cos, sin, state, g, win, wout, mix_layer, ret_layer, stack):
    ns, seg, d = x.shape
    n_ret, _, nh, dk, dv = state.shape
    n = ns * seg
    if stack is None:
        out_spec = pl.BlockSpec((n_ret, 1, nh, dk, dv), lambda i: (0, i, 0, 0, 0))
        extra_specs, extra = [], []
    else:
        out_spec = pl.BlockSpec((None, 1, nh, dk, dv), lambda i: (ret_layer, i, 0, 0, 0))
        extra_specs, extra = [pl.BlockSpec(memory_space=pl.ANY)], [stack]
    out, stack = pl.pallas_call(
        functools.partial(_ret_sample_kernel, seg=seg, layer=ret_layer),
        grid=(ns,),
        in_specs=[_const_spec((n, d)), _const_spec(cos.shape), _const_spec(sin.shape),
                  pl.BlockSpec((None, 1, nh, dk, dv), lambda i: (ret_layer, i, 0, 0, 0)),
                  _layer_spec(g.shape, mix_layer), _const_spec(win.shape),
                  _const_spec(wout.shape)] + extra_specs,
        out_specs=[pl.BlockSpec((n, d), lambda i: (0, 0)), out_spec],
        out_shape=[jax.ShapeDtypeStruct((n, d), F32), jax.ShapeDtypeStruct(state.shape, F32)],
        input_output_aliases={7: 1} if extra else {},
        scratch_shapes=[pltpu.VMEM((n, nh * dk), F32), pltpu.VMEM((n, nh * dk), F32),
                        pltpu.VMEM((n, nh * dv), F32), pltpu.VMEM((n, nh * dv), F32),
                        pltpu.VMEM((n, nh * dv), F32)],
        compiler_params=_params(1),
        name="ret_sample",
    )(x.reshape(n, d), cos, sin, state, g, win, wout, *extra)
    return out.reshape(ns, seg, d), stack


def _pool_mixer(x_ref, nm_ref, wp_ref, ps_ref, xp_ref, bo_ref, pos_start):
    nseg, lt, d = x_ref.shape
    gw = d // len(POOL_WINDOWS)
    x = x_ref[...]
    h = _rms(x, nm_ref[...])
    xp_ref[:, POOL_HIST:, :] = h
    pos = pos_start + lax.broadcasted_iota(jnp.int32, (1, lt, 1), 1)
    ys = []
    for gi, w in enumerate(POOL_WINDOWS):
        cols = slice(gi * gw, (gi + 1) * gw)
        acc = xp_ref[:, :, cols]
        span = 1
        while span < w:
            acc = acc + pltpu.roll(acc, span, axis=1)
            span *= 2
        cnt = jnp.minimum(pos + 1, w).astype(F32)
        dg = (acc[:, POOL_HIST:, :] / cnt - h[:, :, cols]).astype(BF16).reshape(nseg * lt, gw)
        ys.append(_dot(dg, wp_ref[gi]))
    y = jnp.concatenate(ys, axis=-1) * ps_ref[...]
    tail = xp_ref[:, lt:lt + POOL_HIST, :]
    bo_ref[...] = tail
    xp_ref[:, 0:POOL_HIST, :] = tail
    return x.reshape(nseg * lt, d) + y


def _pool_ffn_kernel(xp_ref, bufp_ref, xs_ref, bufs_ref, nm_ref, wp_ref, ps_ref, nf_ref, win_ref,
                     wout_ref, nfin_ref, op_ref, bop_ref, os_ref, bos_ref, hist_p_ref, hist_s_ref,
                     act_ref, *, seq_tiles, sample_pos0, final):
    s = pl.program_id(0)
    last = pl.num_programs(0) - 1

    def tile(x_ref, hist_ref, bo_ref, o_ref, act, pos_start):
        nseg, lt, d = x_ref.shape
        x1 = _pool_mixer(x_ref, nm_ref, wp_ref, ps_ref, hist_ref, bo_ref, pos_start)
        x2 = _ffn(x1, nf_ref[...], win_ref, wout_ref, act)
        if final:
            x2 = _rms(x2, nfin_ref[...])
        o_ref[...] = x2.reshape(nseg, lt, d)

    @pl.when(s < last)
    def _():
        j = s % seq_tiles

        @pl.when(j == 0)
        def _():
            hist_p_ref[:, 0:POOL_HIST, :] = bufp_ref[...]

        tile(xp_ref, hist_p_ref, bop_ref, op_ref, act_ref, j * xp_ref.shape[1])

    @pl.when(s == last)
    def _():
        hist_s_ref[:, 0:POOL_HIST, :] = bufs_ref[...]
        rows = xs_ref.shape[0] * xs_ref.shape[1]
        tile(xs_ref, hist_s_ref, bos_ref, os_ref, act_ref.at[pl.ds(0, rows)], sample_pos0)


def _pool_ffn_call(xp, bufp, xs, bufs, nm, wp, ps, nf, win, wout, nfin, *, mix_layer, pool_layer,
                   tile, sample_pos0, final, cast_weights=()):
    bp, lp, d = xp.shape
    bs, ls, _ = xs.shape
    seq_tiles = lp // tile
    n_tiles = bp * seq_tiles
    assert bs * ls <= tile
    grid = (n_tiles + 1,)
    casts = _Casts(cast_weights, grid)

    def prompt_tile(s):
        t = jnp.minimum(s, n_tiles - 1)
        return t // seq_tiles, t % seq_tiles

    x_spec = pl.BlockSpec((1, tile, d), lambda s: (*prompt_tile(s), 0))
    hist_spec = pl.BlockSpec((1, POOL_HIST, d), lambda s: (prompt_tile(s)[0], 0, 0))
    return pl.pallas_call(
        casts.wrap(functools.partial(_pool_ffn_kernel, seq_tiles=seq_tiles,
                                     sample_pos0=sample_pos0, final=final), 11, 4),
        grid=grid,
        in_specs=[x_spec, hist_spec, _const_spec(xs.shape), _const_spec(bufs.shape),
                  _layer_spec(nm.shape, mix_layer), _layer_spec(wp.shape, pool_layer),
                  _layer_spec(ps.shape, pool_layer), _layer_spec(nf.shape, mix_layer),
                  _const_spec(win.shape), _const_spec(wout.shape),
                  _const_spec(nfin.shape)] + casts.in_specs,
        out_specs=[x_spec, hist_spec, pl.BlockSpec(xs.shape, lambda s: (0, 0, 0)),
                   pl.BlockSpec(bufs.shape, lambda s: (0, 0, 0))] + casts.out_specs,
        out_shape=[jax.ShapeDtypeStruct(xp.shape, F32),
                   jax.ShapeDtypeStruct((bp, POOL_HIST, d), F32),
                   jax.ShapeDtypeStruct(xs.shape, F32),
                   jax.ShapeDtypeStruct(bufs.shape, F32)] + casts.out_shape,
        scratch_shapes=[pltpu.VMEM((1, POOL_HIST + tile, d), F32),
                        pltpu.VMEM((bs, POOL_HIST + ls, d), F32),
                        pltpu.VMEM((tile, wout.shape[0]), BF16)],
        compiler_params=_params(1),
        name="pool_ffn",
    )(xp, bufp, xs, bufs, nm, wp, ps, nf, win, wout, nfin, *casts.operands)


def _rope_angles(pos, half):
    inv = ROPE_BASE ** (-np.arange(half, dtype=np.float64) / half)
    return np.asarray(pos, np.float64)[:, None] * inv[None, :]


def _cos_sin(ang):
    return jnp.asarray(np.cos(ang), F32), jnp.asarray(np.sin(ang), F32)


def kernel(x_prompt, x_sample, state_ret, state_pool, norm_mix, w_ret_in, w_ret_out, w_pool,
           pool_scale, norm_ffn, w_ffn_in, w_ffn_out, norm_final):
    bp, lp, d = x_prompt.shape
    bs, ls, _ = x_sample.shape
    depth = norm_mix.shape[0]
    n_ret, _, nh, dk, dv = state_ret.shape
    assert nh == RET_HEADS and state_pool.shape[2] == POOL_HIST - 1
    assert lp % (PROMPT_TILE * RET_TILES_PER_STEP) == 0 and PROMPT_TILE % RET_CHUNK == 0
    assert lp % POOL_TILE == 0
    assert ls >= POOL_HIST
    assert (bp * lp) % FFN_TILE == 0
    assert depth % 2 == 0

    ret_w = (w_ret_in[0].astype(BF16), w_ret_out[0].astype(BF16))
    w_pool_b = w_pool.astype(BF16)
    norm_mix3 = norm_mix[:, None, :]
    norm_ffn3 = norm_ffn[:, None, :]
    pool_scale3 = pool_scale[:, None, :]
    nfin = norm_final[None, :]

    half = dk // 2
    cosr, sinr = _cos_sin(_rope_angles(np.arange(PROMPT_TILE), half))
    cosb, sinb = _cos_sin(_rope_angles(np.arange(0, lp, PROMPT_TILE), half)[:, None, :])
    rope_p = (cosr, sinr, cosb, sinb)
    cos_s, sin_s = _cos_sin(np.tile(_rope_angles(PAST_LEN + np.arange(ls), half), (bs, 1)))

    pool0_p = jnp.zeros((bp, POOL_HIST, d), state_pool.dtype)
    pool0_s = jnp.pad(state_pool, ((0, 0), (0, 0), (1, 0), (0, 0)))

    xp, xs = x_prompt, x_sample
    ret_p, pool_p, pool_s = [], [], []
    ret_s = None

    def ffn_weights(layer):
        return [(w_ffn_in, layer), (w_ffn_out, layer)]

    for i in range(depth):
        jm = i // 2
        if i % 2 == 0:
            xp, sp, *ffn_w = _ret_prompt_call(xp, rope_p, (nh, dk, dv), norm_mix3, *ret_w, i,
                                              cast_weights=ffn_weights(i))
            ret_p.append(sp)
            xs, ret_s = _ret_sample_call(xs, cos_s, sin_s, state_ret, norm_mix3, *ret_w, i, jm,
                                         ret_s)
            xp, xs, *ffn_w = _ffn_call(xp.reshape(bp * lp, d), xs.reshape(bs * ls, d), norm_ffn3,
                                       *ffn_w, i, FFN_TILE, cast_weights=ffn_weights(i + 1))
            xp, xs = xp.reshape(bp, lp, d), xs.reshape(bs, ls, d)
        else:
            next_ret = [(w_ret_in, jm + 1), (w_ret_out, jm + 1)] if i + 1 < depth else []
            xp, bpn, xs, bsn, *ret_w = _pool_ffn_call(
                xp, pool0_p, xs, pool0_s[jm], norm_mix3, w_pool_b, pool_scale3, norm_ffn3, *ffn_w,
                nfin, mix_layer=i, pool_layer=jm, tile=POOL_TILE, sample_pos0=PAST_LEN,
                final=i == depth - 1, cast_weights=next_ret)
            pool_p.append(bpn[:, 1:, :])
            pool_s.append(bsn[:, 1:, :])
    return (xp, xs, jnp.stack(ret_p), ret_s, jnp.stack(pool_p), jnp.stack(pool_s))
```

```python
import functools
import math

import numpy as np

import jax
import jax.numpy as jnp
from jax import lax
from jax.experimental import pallas as pl
from jax.experimental.pallas import tpu as pltpu

F32 = jnp.float32
BF16 = jnp.bfloat16

EPS = 1e-6
ROPE_BASE = 10000.0
PAST_LEN = 1024
RET_HEADS = 4
POOL_WINDOWS = (2, 4, 8, 16)
POOL_HIST = 16
PROMPT_TILE = 512
RET_CHUNK = 256
RET_TILES_PER_STEP = 2
FFN_TILE = 1024
POOL_TILE = 1024
FFN_CHUNK = 256
BF16_ROWS = 16
VMEM_LIMIT_BYTES = 56 * 1024 * 1024


def _log_gamma(head):
    return math.log1p(-(2.0 ** (-5.0 - head)))


def _rms(x, g):
    return (x * lax.rsqrt(jnp.mean(x * x, axis=-1, keepdims=True) + EPS)) * g


def _dot(a, b):
    return jnp.dot(a, b, preferred_element_type=F32)


def _dot_nt(a, b):
    return lax.dot_general(a, b, (((1,), (1,)), ((), ())), preferred_element_type=F32)


def _dot_tn(a, b):
    return lax.dot_general(a, b, (((0,), (0,)), ((), ())), preferred_element_type=F32)


def _rotary(t, cos, sin):
    half = t.shape[-1] // 2
    t1, t2 = t[:, :half], t[:, half:]
    return jnp.concatenate([t1 * cos - t2 * sin, t1 * sin + t2 * cos], axis=-1)


def _silu(t):
    return t * jax.nn.sigmoid(t)


def _decay_mask(lg, n):
    r = lax.broadcasted_iota(jnp.int32, (n, n), 0)
    c = lax.broadcasted_iota(jnp.int32, (n, n), 1)
    diff = (r - c).astype(F32)
    return jnp.where(diff >= 0, jnp.exp(lg * jnp.maximum(diff, 0.0)), 0.0)


def _ffn(x, nf, win_ref, wout_ref, act_ref):
    d_ff = wout_ref.shape[0]
    h = _rms(x, nf).astype(BF16)
    for lo in range(0, d_ff, FFN_CHUNK):
        up = _dot(h, win_ref[:, d_ff + lo:d_ff + lo + FFN_CHUNK])
        act_ref[:, lo:lo + FFN_CHUNK] = (
            _silu(_dot(h, win_ref[:, lo:lo + FFN_CHUNK])) * up).astype(BF16)
    return x + _dot(act_ref[...], wout_ref[...])


def _const_spec(shape):
    nd = len(shape)
    return pl.BlockSpec(shape, lambda *_: (0,) * nd, pipeline_mode=pl.Buffered(1))


def _layer_spec(stacked_shape, layer):
    nd = len(stacked_shape)
    return pl.BlockSpec((None,) + tuple(stacked_shape[1:]),
                        lambda *_: (layer,) + (0,) * (nd - 1), pipeline_mode=pl.Buffered(1))


def _params(n_axes):
    return pltpu.CompilerParams(dimension_semantics=("arbitrary",) * n_axes,
                                vmem_limit_bytes=VMEM_LIMIT_BYTES)


class _Casts:
    def __init__(self, weights, grid):
        self.grid = grid
        self.in_specs, self.out_specs, self.out_shape, self.operands, self.n_blocks = [], [], [], [], []
        n_steps = math.prod(grid)
        for w, layer in weights:
            _, rows, cols = w.shape
            blk = next(b for b in range(BF16_ROWS, rows + 1, BF16_ROWS)
                       if rows % b == 0 and rows // b <= n_steps)
            n_blk = rows // blk
            self.in_specs.append(pl.BlockSpec(
                (None, blk, cols),
                lambda *g, n_blk=n_blk, layer=layer: (layer, jnp.minimum(self.step(*g), n_blk - 1), 0)))
            self.out_specs.append(pl.BlockSpec(
                (blk, cols), lambda *g, n_blk=n_blk: (jnp.minimum(self.step(*g), n_blk - 1), 0)))
            self.out_shape.append(jax.ShapeDtypeStruct((rows, cols), BF16))
            self.operands.append(w)
            self.n_blocks.append(n_blk)

    def __len__(self):
        return len(self.operands)

    def step(self, *g):
        flat = g[0]
        for idx, extent in zip(g[1:], self.grid[1:]):
            flat = flat * extent + idx
        return flat

    def wrap(self, body, n_in, n_out):
        n = len(self)

        def kernel(*refs):
            ins, srcs = refs[:n_in], refs[n_in:n_in + n]
            outs = refs[n_in + n:n_in + n + n_out]
            dsts = refs[n_in + n + n_out:n_in + 2 * n + n_out]
            body(*ins, *outs, *refs[n_in + 2 * n + n_out:])
            step = self.step(*(pl.program_id(a) for a in range(len(self.grid))))
            for src, dst, n_blk in zip(srcs, dsts, self.n_blocks):
                @pl.when(step < n_blk)
                def _():
                    dst[...] = src[...].astype(BF16)

        return kernel


def _ffn_kernel(xp_ref, xs_ref, nf_ref, win_ref, wout_ref, op_ref, os_ref, act_ref):
    i = pl.program_id(0)
    last = pl.num_programs(0) - 1

    @pl.when(i < last)
    def _():
        op_ref[...] = _ffn(xp_ref[...], nf_ref[...], win_ref, wout_ref, act_ref)

    @pl.when(i == last)
    def _():
        os_ref[...] = _ffn(xs_ref[...], nf_ref[...], win_ref, wout_ref,
                           act_ref.at[pl.ds(0, xs_ref.shape[0])])


def _ffn_call(xp2d, xs2d, nf, win, wout, layer, tile, cast_weights=()):
    n, d = xp2d.shape
    n_tiles = n // tile
    assert xs2d.shape[0] <= tile
    grid = (n_tiles + 1,)
    casts = _Casts(cast_weights, grid)
    prompt_spec = pl.BlockSpec((tile, d), lambda i: (jnp.minimum(i, n_tiles - 1), 0))
    return pl.pallas_call(
        casts.wrap(_ffn_kernel, 5, 2),
        grid=grid,
        in_specs=[prompt_spec, _const_spec(xs2d.shape), _layer_spec(nf.shape, layer),
                  _const_spec(win.shape), _const_spec(wout.shape)] + casts.in_specs,
        out_specs=[prompt_spec, pl.BlockSpec(xs2d.shape, lambda i: (0, 0))] + casts.out_specs,
        out_shape=[jax.ShapeDtypeStruct((n, d), F32),
                   jax.ShapeDtypeStruct(xs2d.shape, F32)] + casts.out_shape,
        scratch_shapes=[pltpu.VMEM((tile, wout.shape[0]), BF16)],
        compiler_params=_params(1),
        name="ffn",
    )(xp2d, xs2d, nf, win, wout, *casts.operands)


def _ret_head_proj(h, win_ref, hd, dk, dv):
    qk = RET_HEADS * dk
    vd = RET_HEADS * dv
    q = _dot(h, win_ref[:, hd * dk:(hd + 1) * dk])
    k = _dot(h, win_ref[:, qk + hd * dk:qk + (hd + 1) * dk]) * (dk ** -0.5)
    v = _dot(h, win_ref[:, 2 * qk + hd * dv:2 * qk + (hd + 1) * dv])
    gate = _dot(h, win_ref[:, 2 * qk + vd + hd * dv:2 * qk + vd + (hd + 1) * dv])
    return q, k, v, gate


def _ret_prompt_kernel(x_ref, cosr_ref, sinr_ref, cosb_ref, sinb_ref, g_ref, win_ref, wout_ref,
                       xo_ref, s_ref, dmask_ref, qdec_ref, kdec_ref, og_ref):
    t = og_ref.shape[0]
    c = dmask_ref.shape[1]
    dk = s_ref.shape[2]
    j = pl.program_id(1)

    @pl.when((pl.program_id(0) == 0) & (j == 0))
    def _():
        row = lax.broadcasted_iota(jnp.int32, (c, dk), 0).astype(F32)
        col = lax.broadcasted_iota(jnp.int32, (dk, c), 1).astype(F32)
        for hd in range(RET_HEADS):
            lg = _log_gamma(hd)
            dmask_ref[hd] = _decay_mask(lg, c)
            qdec_ref[hd] = jnp.exp(lg * (row + 1.0))
            kdec_ref[hd] = jnp.exp(lg * (c - 1.0 - col))

    @pl.when(j == 0)
    def _():
        s_ref[...] = jnp.zeros(s_ref.shape, F32)

    for sub in range(x_ref.shape[1] // t):
        rows = slice(sub * t, (sub + 1) * t)
        xo_ref[0, rows] = _ret_prompt_tile(
            x_ref[0, rows], cosb_ref[sub], sinb_ref[sub], cosr_ref, sinr_ref, g_ref, win_ref,
            wout_ref, s_ref, dmask_ref, qdec_ref, kdec_ref, og_ref)


def _ret_prompt_tile(x, cosb, sinb, cosr_ref, sinr_ref, g_ref, win_ref, wout_ref,
                     s_ref, dmask_ref, qdec_ref, kdec_ref, og_ref):
    t = x.shape[0]
    c = dmask_ref.shape[1]
    dk, dv = s_ref.shape[2], s_ref.shape[3]
    h = _rms(x, g_ref[...]).astype(BF16)
    cosr, sinr = cosr_ref[...], sinr_ref[...]
    cos = cosb * cosr - sinb * sinr
    sin = sinb * cosr + cosb * sinr
    qk, vd = RET_HEADS * dk, RET_HEADS * dv
    n_chunks = t // c
    proj = [dict() for _ in range(RET_HEADS)]
    out = [x]

    def proj_steps(hd):
        def q_step():
            proj[hd]["q"] = _rotary(_dot(h, win_ref[:, hd * dk:(hd + 1) * dk]), cos, sin)

        def k_step():
            k = _dot(h, win_ref[:, qk + hd * dk:qk + (hd + 1) * dk]) * (dk ** -0.5)
            proj[hd]["k_t"] = _rotary(k, cos, sin).T

        def v_step():
            lo = 2 * qk + hd * dv
            proj[hd]["v"] = _dot(h, win_ref[:, lo:lo + dv]).astype(BF16)

        def gate_step():
            lo = 2 * qk + vd + hd * dv
            proj[hd]["gate"] = _dot(h, win_ref[:, lo:lo + dv])

        return [q_step, k_step, v_step, gate_step]

    def out_step(hd):
        def step():
            out[0] = out[0] + _dot(og_ref[:, hd * dv:(hd + 1) * dv],
                                   wout_ref[hd * dv:(hd + 1) * dv, :])
        return step

    def core(hd, fillers):
        lg = _log_gamma(hd)
        q, k_t, v, gate = (proj[hd][name] for name in ("q", "k_t", "v", "gate"))
        chunks = [slice(ci * c, (ci + 1) * c) for ci in range(n_chunks)]
        scores = [_dot(q[rows].astype(BF16), k_t[:, rows].astype(BF16)) for rows in chunks]
        fillers.pop(0)()
        s_cur = s_ref[0, hd]
        for rows, sc in zip(chunks, scores):
            vc = v[rows]
            q_dec = (q[rows] * qdec_ref[hd]).astype(BF16)
            k_dec_t = (k_t[:, rows] * kdec_ref[hd]).astype(BF16)
            lhs = jnp.concatenate([(sc * dmask_ref[hd]).astype(BF16), q_dec], axis=1)
            rhs = jnp.concatenate([vc, s_cur.astype(BF16)], axis=0)
            o = _dot(lhs, rhs)
            s_cur = math.exp(lg * c) * s_cur + _dot(k_dec_t, vc)
            if fillers:
                fillers.pop(0)()
            o = o * lax.rsqrt(jnp.mean(o * o, axis=-1, keepdims=True) + EPS)
            og_ref[rows, hd * dv:(hd + 1) * dv] = (_silu(gate[rows]) * o).astype(BF16)
        s_ref[0, hd] = s_cur
        for filler in fillers:
            filler()

    for step in proj_steps(0):
        step()
    for hd in range(RET_HEADS):
        if hd + 1 < RET_HEADS:
            fillers = proj_steps(hd + 1)
        else:
            fillers = [out_step(i) for i in range(hd)]
        core(hd, fillers)
    out_step(RET_HEADS - 1)()
    return out[0]


def _ret_prompt_call(x, rope, s_shape, g, win, wout, mix_layer, cast_weights=()):
    b, l, d = x.shape
    nh, dk, dv = s_shape
    t, c = PROMPT_TILE, RET_CHUNK
    cosr, sinr, cosb, sinb = rope
    half = cosr.shape[1]
    n_sub = RET_TILES_PER_STEP
    base_spec = pl.BlockSpec((n_sub, 1, half), lambda i, j: (j, 0, 0))
    grid = (b, l // (n_sub * t))
    casts = _Casts(cast_weights, grid)
    return pl.pallas_call(
        casts.wrap(_ret_prompt_kernel, 8, 2),
        grid=grid,
        in_specs=[pl.BlockSpec((1, n_sub * t, d), lambda i, j: (i, j, 0)),
                  _const_spec(cosr.shape), _const_spec(sinr.shape), base_spec, base_spec,
                  _layer_spec(g.shape, mix_layer), _const_spec(win.shape),
                  _const_spec(wout.shape)] + casts.in_specs,
        out_specs=[pl.BlockSpec((1, n_sub * t, d), lambda i, j: (i, j, 0)),
                   pl.BlockSpec((1, nh, dk, dv), lambda i, j: (i, 0, 0, 0))] + casts.out_specs,
        out_shape=[jax.ShapeDtypeStruct((b, l, d), F32),
                   jax.ShapeDtypeStruct((b, nh, dk, dv), F32)] + casts.out_shape,
        scratch_shapes=[pltpu.VMEM((nh, c, c), F32), pltpu.VMEM((nh, c, dk), F32),
                        pltpu.VMEM((nh, dk, c), F32), pltpu.VMEM((t, nh * dv), BF16)],
        compiler_params=_params(2),
        name="ret_prompt",
    )(x, cosr, sinr, cosb, sinb, g, win, wout, *casts.operands)


def _own_state_slab(s_ref, layer):
    if s_ref.ndim == 4:
        return s_ref
    for other in range(s_ref.shape[0]):
        if other != layer:
            s_ref[other] = jnp.zeros(s_ref.shape[1:], F32)
    return s_ref.at[layer]


def _ret_sample_kernel(x_ref, cos_ref, sin_ref, s0_ref, g_ref, win_ref, wout_ref, *rest, seg,
                       layer):
    xo_ref, s_ref, q_ref, k_ref, v_ref, gate_ref, og_ref = rest[-7:]
    s_ref = _own_state_slab(s_ref, layer)
    dk, dv = s_ref.shape[2], s_ref.shape[3]
    s = pl.program_id(0)

    @pl.when(s == 0)
    def _():
        h = _rms(x_ref[...], g_ref[...]).astype(BF16)
        cos, sin = cos_ref[...], sin_ref[...]
        for hd in range(RET_HEADS):
            q, k, v, gate = _ret_head_proj(h, win_ref, hd, dk, dv)
            q_ref[:, hd * dk:(hd + 1) * dk] = _rotary(q, cos, sin)
            k_ref[:, hd * dk:(hd + 1) * dk] = _rotary(k, cos, sin)
            v_ref[:, hd * dv:(hd + 1) * dv] = v
            gate_ref[:, hd * dv:(hd + 1) * dv] = gate

    rows = pl.ds(pl.multiple_of(s * seg, seg), seg)
    idx = lax.broadcasted_iota(jnp.int32, (seg, 1), 0).astype(F32)
    for hd in range(RET_HEADS):
        lg = _log_gamma(hd)
        q = q_ref[rows, hd * dk:(hd + 1) * dk]
        k = k_ref[rows, hd * dk:(hd + 1) * dk]
        v = v_ref[rows, hd * dv:(hd + 1) * dv].astype(BF16)
        gate = gate_ref[rows, hd * dv:(hd + 1) * dv]
        q_dec = (q * jnp.exp(lg * (idx + 1.0))).astype(BF16)
        k_dec = (k * jnp.exp(lg * (seg - 1.0 - idx))).astype(BF16)
        scores = _dot_nt(q.astype(BF16), k.astype(BF16)) * _decay_mask(lg, seg)
        s_old = s0_ref[0, hd]
        o = _dot(scores.astype(BF16), v) + _dot(q_dec, s_old.astype(BF16))
        s_ref[0, hd] = math.exp(lg * seg) * s_old + _dot_tn(k_dec, v)
        o = o * lax.rsqrt(jnp.mean(o * o, axis=-1, keepdims=True) + EPS)
        og_ref[rows, hd * dv:(hd + 1) * dv] = _silu(gate) * o

    @pl.when(s == pl.num_programs(0) - 1)
    def _():
        xo_ref[...] = x_ref[...] + _dot(og_ref[...].astype(BF16), wout_ref[...])


def _ret_sample_call(x, cos, sin, state, g, win, wout, mix_layer, ret_layer, stack):
    ns, seg, d = x.shape
    n_ret, _, nh, dk, dv = state.shape
    n = ns * seg
    if stack is None:
        out_spec = pl.BlockSpec((n_ret, 1, nh, dk, dv), lambda i: (0, i, 0, 0, 0))
        extra_specs, extra = [], []
    else:
        out_spec = pl.BlockSpec((None, 1, nh, dk, dv), lambda i: (ret_layer, i, 0, 0, 0))
        extra_specs, extra = [pl.BlockSpec(memory_space=pl.ANY)], [stack]
    out, stack = pl.pallas_call(
        functools.partial(_ret_sample_kernel, seg=seg, layer=ret_layer),
        grid=(ns,),
        in_specs=[_const_spec((n, d)), _const_spec(cos.shape), _const_spec(sin.shape),
                  pl.BlockSpec((None, 1, nh, dk, dv), lambda i: (ret_layer, i, 0, 0, 0)),
                  _layer_spec(g.shape, mix_layer), _const_spec(win.shape),
                  _const_spec(wout.shape)] + extra_specs,
        out_specs=[pl.BlockSpec((n, d), lambda i: (0, 0)), out_spec],
        out_shape=[jax.ShapeDtypeStruct((n, d), F32), jax.ShapeDtypeStruct(state.shape, F32)],
        input_output_aliases={7: 1} if extra else {},
        scratch_shapes=[pltpu.VMEM((n, nh * dk), F32), pltpu.VMEM((n, nh * dk), F32),
                        pltpu.VMEM((n, nh * dv), F32), pltpu.VMEM((n, nh * dv), F32),
                        pltpu.VMEM((n, nh * dv), F32)],
        compiler_params=_params(1),
        name="ret_sample",
    )(x.reshape(n, d), cos, sin, state, g, win, wout, *extra)
    return out.reshape(ns, seg, d), stack


def _pool_mixer(x_ref, nm_ref, wp_ref, ps_ref, xp_ref, bo_ref, pos_start):
    nseg, lt, d = x_ref.shape
    gw = d // len(POOL_WINDOWS)
    x = x_ref[...]
    h = _rms(x, nm_ref[...])
    xp_ref[:, POOL_HIST:, :] = h
    pos = pos_start + lax.broadcasted_iota(jnp.int32, (1, lt, 1), 1)
    ys = []
    for gi, w in enumerate(POOL_WINDOWS):
        cols = slice(gi * gw, (gi + 1) * gw)
        acc = xp_ref[:, :, cols]
        span = 1
        while span < w:
            acc = acc + pltpu.roll(acc, span, axis=1)
            span *= 2
        inv_cnt = 1.0 / jnp.minimum(pos + 1, w).astype(F32)
        dg = (acc[:, POOL_HIST:, :] * inv_cnt - h[:, :, cols]).astype(BF16).reshape(nseg * lt, gw)
        ys.append(_dot(dg, wp_ref[gi]))
    y = jnp.concatenate(ys, axis=-1) * ps_ref[...]
    tail = xp_ref[:, lt:lt + POOL_HIST, :]
    bo_ref[...] = tail
    xp_ref[:, 0:POOL_HIST, :] = tail
    return x.reshape(nseg * lt, d) + y


def _pool_ffn_kernel(xp_ref, bufp_ref, xs_ref, bufs_ref, nm_ref, wp_ref, ps_ref, nf_ref, win_ref,
                     wout_ref, nfin_ref, op_ref, bop_ref, os_ref, bos_ref, hist_p_ref, hist_s_ref,
                     act_ref, *, seq_tiles, sample_pos0, final):
    s = pl.program_id(0)
    last = pl.num_programs(0) - 1

    def tile(x_ref, hist_ref, bo_ref, o_ref, act, pos_start):
        nseg, lt, d = x_ref.shape
        x1 = _pool_mixer(x_ref, nm_ref, wp_ref, ps_ref, hist_ref, bo_ref, pos_start)
        x2 = _ffn(x1, nf_ref[...], win_ref, wout_ref, act)
        if final:
            x2 = _rms(x2, nfin_ref[...])
        o_ref[...] = x2.reshape(nseg, lt, d)

    @pl.when(s < last)
    def _():
        j = s % seq_tiles

        @pl.when(j == 0)
        def _():
            hist_p_ref[:, 0:POOL_HIST, :] = bufp_ref[...]

        tile(xp_ref, hist_p_ref, bop_ref, op_ref, act_ref, j * xp_ref.shape[1])

    @pl.when(s == last)
    def _():
        hist_s_ref[:, 0:POOL_HIST, :] = bufs_ref[...]
        rows = xs_ref.shape[0] * xs_ref.shape[1]
        tile(xs_ref, hist_s_ref, bos_ref, os_ref, act_ref.at[pl.ds(0, rows)], sample_pos0)


def _pool_ffn_call(xp, bufp, xs, bufs, nm, wp, ps, nf, win, wout, nfin, *, mix_layer, pool_layer,
                   tile, sample_pos0, final, cast_weights=()):
    bp, lp, d = xp.shape
    bs, ls, _ = xs.shape
    seq_tiles = lp // tile
    n_tiles = bp * seq_tiles
    assert bs * ls <= tile
    grid = (n_tiles + 1,)
    casts = _Casts(cast_weights, grid)

    def prompt_tile(s):
        t = jnp.minimum(s, n_tiles - 1)
        return t // seq_tiles, t % seq_tiles

    x_spec = pl.BlockSpec((1, tile, d), lambda s: (*prompt_tile(s), 0))
    hist_spec = pl.BlockSpec((1, POOL_HIST, d), lambda s: (prompt_tile(s)[0], 0, 0))
    return pl.pallas_call(
        casts.wrap(functools.partial(_pool_ffn_kernel, seq_tiles=seq_tiles,
                                     sample_pos0=sample_pos0, final=final), 11, 4),
        grid=grid,
        in_specs=[x_spec, hist_spec, _const_spec(xs.shape), _const_spec(bufs.shape),
                  _layer_spec(nm.shape, mix_layer), _layer_spec(wp.shape, pool_layer),
                  _layer_spec(ps.shape, pool_layer), _layer_spec(nf.shape, mix_layer),
                  _const_spec(win.shape), _const_spec(wout.shape),
                  _const_spec(nfin.shape)] + casts.in_specs,
        out_specs=[x_spec, hist_spec, pl.BlockSpec(xs.shape, lambda s: (0, 0, 0)),
                   pl.BlockSpec(bufs.shape, lambda s: (0, 0, 0))] + casts.out_specs,
        out_shape=[jax.ShapeDtypeStruct(xp.shape, F32),
                   jax.ShapeDtypeStruct((bp, POOL_HIST, d), F32),
                   jax.ShapeDtypeStruct(xs.shape, F32),
                   jax.ShapeDtypeStruct(bufs.shape, F32)] + casts.out_shape,
        scratch_shapes=[pltpu.VMEM((1, POOL_HIST + tile, d), F32),
                        pltpu.VMEM((bs, POOL_HIST + ls, d), F32),
                        pltpu.VMEM((tile, wout.shape[0]), BF16)],
        compiler_params=_params(1),
        name="pool_ffn",
    )(xp, bufp, xs, bufs, nm, wp, ps, nf, win, wout, nfin, *casts.operands)


def _rope_angles(pos, half):
    inv = ROPE_BASE ** (-np.arange(half, dtype=np.float64) / half)
    return np.asarray(pos, np.float64)[:, None] * inv[None, :]


def _cos_sin(ang):
    return jnp.asarray(np.cos(ang), F32), jnp.asarray(np.sin(ang), F32)


def kernel(x_prompt, x_sample, state_ret, state_pool, norm_mix, w_ret_in, w_ret_out, w_pool,
           pool_scale, norm_ffn, w_ffn_in, w_ffn_out, norm_final):
    bp, lp, d = x_prompt.shape
    bs, ls, _ = x_sample.shape
    depth = norm_mix.shape[0]
    n_ret, _, nh, dk, dv = state_ret.shape
    assert nh == RET_HEADS and state_pool.shape[2] == POOL_HIST - 1
    assert lp % (PROMPT_TILE * RET_TILES_PER_STEP) == 0 and PROMPT_TILE % RET_CHUNK == 0
    assert lp % POOL_TILE == 0
    assert ls >= POOL_HIST
    assert (bp * lp) % FFN_TILE == 0
    assert depth % 2 == 0

    ret_w = (w_ret_in[0].astype(BF16), w_ret_out[0].astype(BF16))
    w_pool_b = w_pool.astype(BF16)
    norm_mix3 = norm_mix[:, None, :]
    norm_ffn3 = norm_ffn[:, None, :]
    pool_scale3 = pool_scale[:, None, :]
    nfin = norm_final[None, :]

    half = dk // 2
    cosr, sinr = _cos_sin(_rope_angles(np.arange(PROMPT_TILE), half))
    cosb, sinb = _cos_sin(_rope_angles(np.arange(0, lp, PROMPT_TILE), half)[:, None, :])
    rope_p = (cosr, sinr, cosb, sinb)
    cos_s, sin_s = _cos_sin(np.tile(_rope_angles(PAST_LEN + np.arange(ls), half), (bs, 1)))

    pool0_p = jnp.zeros((bp, POOL_HIST, d), state_pool.dtype)
    pool0_s = jnp.pad(state_pool, ((0, 0), (0, 0), (1, 0), (0, 0)))

    xp, xs = x_prompt, x_sample
    ret_p, pool_p, pool_s = [], [], []
    ret_s = None

    def ffn_weights(layer):
        return [(w_ffn_in, layer), (w_ffn_out, layer)]

    for i in range(depth):
        jm = i // 2
        if i % 2 == 0:
            xp, sp, *ffn_w = _ret_prompt_call(xp, rope_p, (nh, dk, dv), norm_mix3, *ret_w, i,
                                              cast_weights=ffn_weights(i))
            ret_p.append(sp)
            xs, ret_s = _ret_sample_call(xs, cos_s, sin_s, state_ret, norm_mix3, *ret_w, i, jm,
                                         ret_s)
            xp, xs, *ffn_w = _ffn_call(xp.reshape(bp * lp, d), xs.reshape(bs * ls, d), norm_ffn3,
                                       *ffn_w, i, FFN_TILE, cast_weights=ffn_weights(i + 1))
            xp, xs = xp.reshape(bp, lp, d), xs.reshape(bs, ls, d)
        else:
            next_ret = [(w_ret_in, jm + 1), (w_ret_out, jm + 1)] if i + 1 < depth else []
            xp, bpn, xs, bsn, *ret_w = _pool_ffn_call(
                xp, pool0_p, xs, pool0_s[jm], norm_mix3, w_pool_b, pool_scale3, norm_ffn3, *ffn_w,
                nfin, mix_layer=i, pool_layer=jm, tile=POOL_TILE, sample_pos0=PAST_LEN,
                final=i == depth - 1, cast_weights=next_ret)
            pool_p.append(bpn[:, 1:, :])
            pool_s.append(bsn[:, 1:, :])
    return (xp, xs, jnp.stack(ret_p), ret_s, jnp.stack(pool_p), jnp.stack(pool_s))
```

```python
import functools
import math

import numpy as np

import jax
import jax.numpy as jnp
from jax import lax
from jax.experimental import pallas as pl
from jax.experimental.pallas import tpu as pltpu

F32 = jnp.float32
BF16 = jnp.bfloat16

EPS = 1e-6
ROPE_BASE = 10000.0
PAST_LEN = 1024
RET_HEADS = 4
POOL_WINDOWS = (2, 4, 8, 16)
POOL_HIST = 16
PROMPT_TILE = 512
RET_CHUNK = 256
RET_TILES_PER_STEP = 2
FFN_TILE = 1024
POOL_TILE = 1024
FFN_CHUNK = 256
BF16_ROWS = 16
VMEM_LIMIT_BYTES = 56 * 1024 * 1024


def _log_gamma(head):
    return math.log1p(-(2.0 ** (-5.0 - head)))


def _rms(x, g):
    return (x * lax.rsqrt(jnp.mean(x * x, axis=-1, keepdims=True) + EPS)) * g


def _dot(a, b):
    return jnp.dot(a, b, preferred_element_type=F32)


def _dot_nt(a, b):
    return lax.dot_general(a, b, (((1,), (1,)), ((), ())), preferred_element_type=F32)


def _dot_tn(a, b):
    return lax.dot_general(a, b, (((0,), (0,)), ((), ())), preferred_element_type=F32)


def _rotary(t, cos, sin):
    half = t.shape[-1] // 2
    t1, t2 = t[:, :half], t[:, half:]
    return jnp.concatenate([t1 * cos - t2 * sin, t1 * sin + t2 * cos], axis=-1)


def _silu(t):
    return t * jax.nn.sigmoid(t)


def _decay_mask(lg, n):
    r = lax.broadcasted_iota(jnp.int32, (n, n), 0)
    c = lax.broadcasted_iota(jnp.int32, (n, n), 1)
    diff = (r - c).astype(F32)
    return jnp.where(diff >= 0, jnp.exp(lg * jnp.maximum(diff, 0.0)), 0.0)


def _ffn(x, nf, win_ref, wout_ref, act_ref):
    d_ff = wout_ref.shape[0]
    h = _rms(x, nf).astype(BF16)
    for lo in range(0, d_ff, FFN_CHUNK):
        up = _dot(h, win_ref[:, d_ff + lo:d_ff + lo + FFN_CHUNK])
        act_ref[:, lo:lo + FFN_CHUNK] = (
            _silu(_dot(h, win_ref[:, lo:lo + FFN_CHUNK])) * up).astype(BF16)
    return x + _dot(act_ref[...], wout_ref[...])


def _ffn_refs(x_ref, o_ref, nf, win_ref, wout_ref, act_ref):
    d_ff = wout_ref.shape[0]
    h = _rms(x_ref[...], nf).astype(BF16)
    for lo in range(0, d_ff, FFN_CHUNK):
        up = _dot(h, win_ref[:, d_ff + lo:d_ff + lo + FFN_CHUNK])
        act_ref[:, lo:lo + FFN_CHUNK] = (
            _silu(_dot(h, win_ref[:, lo:lo + FFN_CHUNK])) * up).astype(BF16)
    for lo in range(0, wout_ref.shape[1], FFN_CHUNK):
        cols = slice(lo, lo + FFN_CHUNK)
        o_ref[:, cols] = x_ref[:, cols] + _dot(act_ref[...], wout_ref[:, cols])


def _const_spec(shape):
    nd = len(shape)
    return pl.BlockSpec(shape, lambda *_: (0,) * nd, pipeline_mode=pl.Buffered(1))


def _layer_spec(stacked_shape, layer):
    nd = len(stacked_shape)
    return pl.BlockSpec((None,) + tuple(stacked_shape[1:]),
                        lambda *_: (layer,) + (0,) * (nd - 1), pipeline_mode=pl.Buffered(1))


def _params(n_axes):
    return pltpu.CompilerParams(dimension_semantics=("arbitrary",) * n_axes,
                                vmem_limit_bytes=VMEM_LIMIT_BYTES)


class _Casts:
    def __init__(self, weights, grid):
        self.grid = grid
        self.in_specs, self.out_specs, self.out_shape, self.operands, self.n_blocks = [], [], [], [], []
        n_steps = math.prod(grid)
        for w, layer in weights:
            _, rows, cols = w.shape
            blk = next(b for b in range(BF16_ROWS, rows + 1, BF16_ROWS)
                       if rows % b == 0 and rows // b <= n_steps)
            n_blk = rows // blk
            self.in_specs.append(pl.BlockSpec(
                (None, blk, cols),
                lambda *g, n_blk=n_blk, layer=layer: (layer, jnp.minimum(self.step(*g), n_blk - 1), 0)))
            self.out_specs.append(pl.BlockSpec(
                (blk, cols), lambda *g, n_blk=n_blk: (jnp.minimum(self.step(*g), n_blk - 1), 0)))
            self.out_shape.append(jax.ShapeDtypeStruct((rows, cols), BF16))
            self.operands.append(w)
            self.n_blocks.append(n_blk)

    def __len__(self):
        return len(self.operands)

    def step(self, *g):
        flat = g[0]
        for idx, extent in zip(g[1:], self.grid[1:]):
            flat = flat * extent + idx
        return flat

    def wrap(self, body, n_in, n_out):
        n = len(self)

        def kernel(*refs):
            ins, srcs = refs[:n_in], refs[n_in:n_in + n]
            outs = refs[n_in + n:n_in + n + n_out]
            dsts = refs[n_in + n + n_out:n_in + 2 * n + n_out]
            body(*ins, *outs, *refs[n_in + 2 * n + n_out:])
            step = self.step(*(pl.program_id(a) for a in range(len(self.grid))))
            for src, dst, n_blk in zip(srcs, dsts, self.n_blocks):
                @pl.when(step < n_blk)
                def _():
                    dst[...] = src[...].astype(BF16)

        return kernel


def _ffn_kernel(xp_ref, xs_ref, nf_ref, win_ref, wout_ref, op_ref, os_ref, act_ref):
    i = pl.program_id(0)
    last = pl.num_programs(0) - 1

    @pl.when(i < last)
    def _():
        _ffn_refs(xp_ref, op_ref, nf_ref[...], win_ref, wout_ref, act_ref)

    @pl.when(i == last)
    def _():
        os_ref[...] = _ffn(xs_ref[...], nf_ref[...], win_ref, wout_ref,
                           act_ref.at[pl.ds(0, xs_ref.shape[0])])


def _ffn_call(xp2d, xs2d, nf, win, wout, layer, tile, cast_weights=()):
    n, d = xp2d.shape
    n_tiles = n // tile
    assert xs2d.shape[0] <= tile
    grid = (n_tiles + 1,)
    casts = _Casts(cast_weights, grid)
    prompt_spec = pl.BlockSpec((tile, d), lambda i: (jnp.minimum(i, n_tiles - 1), 0))
    return pl.pallas_call(
        casts.wrap(_ffn_kernel, 5, 2),
        grid=grid,
        in_specs=[prompt_spec, _const_spec(xs2d.shape), _layer_spec(nf.shape, layer),
                  _const_spec(win.shape), _const_spec(wout.shape)] + casts.in_specs,
        out_specs=[prompt_spec, pl.BlockSpec(xs2d.shape, lambda i: (0, 0))] + casts.out_specs,
        out_shape=[jax.ShapeDtypeStruct((n, d), F32),
                   jax.ShapeDtypeStruct(xs2d.shape, F32)] + casts.out_shape,
        scratch_shapes=[pltpu.VMEM((tile, wout.shape[0]), BF16)],
        compiler_params=_params(1),
        name="ffn",
    )(xp2d, xs2d, nf, win, wout, *casts.operands)


def _ret_head_proj(h, win_ref, hd, dk, dv):
    qk = RET_HEADS * dk
    vd = RET_HEADS * dv
    q = _dot(h, win_ref[:, hd * dk:(hd + 1) * dk])
    k = _dot(h, win_ref[:, qk + hd * dk:qk + (hd + 1) * dk]) * (dk ** -0.5)
    v = _dot(h, win_ref[:, 2 * qk + hd * dv:2 * qk + (hd + 1) * dv])
    gate = _dot(h, win_ref[:, 2 * qk + vd + hd * dv:2 * qk + vd + (hd + 1) * dv])
    return q, k, v, gate


def _ret_prompt_kernel(x_ref, cosr_ref, sinr_ref, cosb_ref, sinb_ref, g_ref, win_ref, wout_ref,
                       xo_ref, s_ref, dmask_ref, qdec_ref, kdec_ref, og_ref):
    t = og_ref.shape[0]
    c = dmask_ref.shape[1]
    dk = s_ref.shape[2]
    j = pl.program_id(1)

    @pl.when((pl.program_id(0) == 0) & (j == 0))
    def _():
        row = lax.broadcasted_iota(jnp.int32, (c, dk), 0).astype(F32)
        col = lax.broadcasted_iota(jnp.int32, (dk, c), 1).astype(F32)
        for hd in range(RET_HEADS):
            lg = _log_gamma(hd)
            dmask_ref[hd] = _decay_mask(lg, c)
            qdec_ref[hd] = jnp.exp(lg * (row + 1.0))
            kdec_ref[hd] = jnp.exp(lg * (c - 1.0 - col))

    @pl.when(j == 0)
    def _():
        s_ref[...] = jnp.zeros(s_ref.shape, F32)

    for sub in range(x_ref.shape[1] // t):
        rows = slice(sub * t, (sub + 1) * t)
        xo_ref[0, rows] = _ret_prompt_tile(
            x_ref[0, rows], cosb_ref[sub], sinb_ref[sub], cosr_ref, sinr_ref, g_ref, win_ref,
            wout_ref, s_ref, dmask_ref, qdec_ref, kdec_ref, og_ref)


def _ret_prompt_tile(x, cosb, sinb, cosr_ref, sinr_ref, g_ref, win_ref, wout_ref,
                     s_ref, dmask_ref, qdec_ref, kdec_ref, og_ref):
    t = x.shape[0]
    c = dmask_ref.shape[1]
    dk, dv = s_ref.shape[2], s_ref.shape[3]
    h = _rms(x, g_ref[...]).astype(BF16)
    cosr, sinr = cosr_ref[...], sinr_ref[...]
    cos = cosb * cosr - sinb * sinr
    sin = sinb * cosr + cosb * sinr
    qk, vd = RET_HEADS * dk, RET_HEADS * dv
    n_chunks = t // c
    proj = [dict() for _ in range(RET_HEADS)]
    out = [x]

    def proj_steps(hd):
        def q_step():
            proj[hd]["q"] = _rotary(_dot(h, win_ref[:, hd * dk:(hd + 1) * dk]), cos, sin)

        def k_step():
            k = _dot(h, win_ref[:, qk + hd * dk:qk + (hd + 1) * dk]) * (dk ** -0.5)
            proj[hd]["k_t"] = _rotary(k, cos, sin).T

        def v_step():
            lo = 2 * qk + hd * dv
            proj[hd]["v"] = _dot(h, win_ref[:, lo:lo + dv]).astype(BF16)

        def gate_step():
            lo = 2 * qk + vd + hd * dv
            proj[hd]["gate"] = _dot(h, win_ref[:, lo:lo + dv])

        return [q_step, k_step, v_step, gate_step]

    def out_step(hd):
        def step():
            out[0] = out[0] + _dot(og_ref[:, hd * dv:(hd + 1) * dv],
                                   wout_ref[hd * dv:(hd + 1) * dv, :])
        return step

    def core(hd, fillers):
        lg = _log_gamma(hd)
        q, k_t, v, gate = (proj[hd][name] for name in ("q", "k_t", "v", "gate"))
        chunks = [slice(ci * c, (ci + 1) * c) for ci in range(n_chunks)]
        scores = [_dot(q[rows].astype(BF16), k_t[:, rows].astype(BF16)) for rows in chunks]
        fillers.pop(0)()
        s_cur = s_ref[0, hd]
        for rows, sc in zip(chunks, scores):
            vc = v[rows]
            q_dec = (q[rows] * qdec_ref[hd]).astype(BF16)
            k_dec_t = (k_t[:, rows] * kdec_ref[hd]).astype(BF16)
            lhs = jnp.concatenate([(sc * dmask_ref[hd]).astype(BF16), q_dec], axis=1)
            rhs = jnp.concatenate([vc, s_cur.astype(BF16)], axis=0)
            o = _dot(lhs, rhs)
            s_cur = math.exp(lg * c) * s_cur + _dot(k_dec_t, vc)
            if fillers:
                fillers.pop(0)()
            o = o * lax.rsqrt(jnp.mean(o * o, axis=-1, keepdims=True) + EPS)
            og_ref[rows, hd * dv:(hd + 1) * dv] = (_silu(gate[rows]) * o).astype(BF16)
        s_ref[0, hd] = s_cur
        for filler in fillers:
            filler()

    for step in proj_steps(0):
        step()
    for hd in range(RET_HEADS):
        if hd + 1 < RET_HEADS:
            fillers = proj_steps(hd + 1)
        else:
            fillers = [out_step(i) for i in range(hd)]
        core(hd, fillers)
    out_step(RET_HEADS - 1)()
    return out[0]


def _ret_prompt_call(x, rope, s_shape, g, win, wout, mix_layer, cast_weights=()):
    b, l, d = x.shape
    nh, dk, dv = s_shape
    t, c = PROMPT_TILE, RET_CHUNK
    cosr, sinr, cosb, sinb = rope
    half = cosr.shape[1]
    n_sub = RET_TILES_PER_STEP
    base_spec = pl.BlockSpec((n_sub, 1, half), lambda i, j: (j, 0, 0))
    grid = (b, l // (n_sub * t))
    casts = _Casts(cast_weights, grid)
    return pl.pallas_call(
        casts.wrap(_ret_prompt_kernel, 8, 2),
        grid=grid,
        in_specs=[pl.BlockSpec((1, n_sub * t, d), lambda i, j: (i, j, 0)),
                  _const_spec(cosr.shape), _const_spec(sinr.shape), base_spec, base_spec,
                  _layer_spec(g.shape, mix_layer), _const_spec(win.shape),
                  _const_spec(wout.shape)] + casts.in_specs,
        out_specs=[pl.BlockSpec((1, n_sub * t, d), lambda i, j: (i, j, 0)),
                   pl.BlockSpec((1, nh, dk, dv), lambda i, j: (i, 0, 0, 0))] + casts.out_specs,
        out_shape=[jax.ShapeDtypeStruct((b, l, d), F32),
                   jax.ShapeDtypeStruct((b, nh, dk, dv), F32)] + casts.out_shape,
        scratch_shapes=[pltpu.VMEM((nh, c, c), F32), pltpu.VMEM((nh, c, dk), F32),
                        pltpu.VMEM((nh, dk, c), F32), pltpu.VMEM((t, nh * dv), BF16)],
        compiler_params=_params(2),
        name="ret_prompt",
    )(x, cosr, sinr, cosb, sinb, g, win, wout, *casts.operands)


def _own_state_slab(s_ref, layer):
    if s_ref.ndim == 4:
        return s_ref
    for other in range(s_ref.shape[0]):
        if other != layer:
            s_ref[other] = jnp.zeros(s_ref.shape[1:], F32)
    return s_ref.at[layer]


def _ret_sample_kernel(x_ref, cos_ref, sin_ref, s0_ref, g_ref, win_ref, wout_ref, *rest, seg,
                       layer):
    xo_ref, s_ref, q_ref, k_ref, v_ref, gate_ref, og_ref = rest[-7:]
    s_ref = _own_state_slab(s_ref, layer)
    dk, dv = s_ref.shape[2], s_ref.shape[3]
    s = pl.program_id(0)

    @pl.when(s == 0)
    def _():
        h = _rms(x_ref[...], g_ref[...]).astype(BF16)
        cos, sin = cos_ref[...], sin_ref[...]
        for hd in range(RET_HEADS):
            q, k, v, gate = _ret_head_proj(h, win_ref, hd, dk, dv)
            q_ref[:, hd * dk:(hd + 1) * dk] = _rotary(q, cos, sin)
            k_ref[:, hd * dk:(hd + 1) * dk] = _rotary(k, cos, sin)
            v_ref[:, hd * dv:(hd + 1) * dv] = v
            gate_ref[:, hd * dv:(hd + 1) * dv] = gate

    rows = pl.ds(pl.multiple_of(s * seg, seg), seg)
    idx = lax.broadcasted_iota(jnp.int32, (seg, 1), 0).astype(F32)
    for hd in range(RET_HEADS):
        lg = _log_gamma(hd)
        q = q_ref[rows, hd * dk:(hd + 1) * dk]
        k = k_ref[rows, hd * dk:(hd + 1) * dk]
        v = v_ref[rows, hd * dv:(hd + 1) * dv].astype(BF16)
        gate = gate_ref[rows, hd * dv:(hd + 1) * dv]
        q_dec = (q * jnp.exp(lg * (idx + 1.0))).astype(BF16)
        k_dec = (k * jnp.exp(lg * (seg - 1.0 - idx))).astype(BF16)
        scores = _dot_nt(q.astype(BF16), k.astype(BF16)) * _decay_mask(lg, seg)
        s_old = s0_ref[0, hd]
        o = _dot(scores.astype(BF16), v) + _dot(q_dec, s_old.astype(BF16))
        s_ref[0, hd] = math.exp(lg * seg) * s_old + _dot_tn(k_dec, v)
        o = o * lax.rsqrt(jnp.mean(o * o, axis=-1, keepdims=True) + EPS)
        og_ref[rows, hd * dv:(hd + 1) * dv] = _silu(gate) * o

    @pl.when(s == pl.num_programs(0) - 1)
    def _():
        xo_ref[...] = x_ref[...] + _dot(og_ref[...].astype(BF16), wout_ref[...])


def _ret_sample_call(x, cos, sin, state, g, win, wout, mix_layer, ret_layer, stack):
    ns, seg, d = x.shape
    n_ret, _, nh, dk, dv = state.shape
    n = ns * seg
    if stack is None:
        out_spec = pl.BlockSpec((n_ret, 1, nh, dk, dv), lambda i: (0, i, 0, 0, 0))
        extra_specs, extra = [], []
    else:
        out_spec = pl.BlockSpec((None, 1, nh, dk, dv), lambda i: (ret_layer, i, 0, 0, 0))
        extra_specs, extra = [pl.BlockSpec(memory_space=pl.ANY)], [stack]
    out, stack = pl.pallas_call(
        functools.partial(_ret_sample_kernel, seg=seg, layer=ret_layer),
        grid=(ns,),
        in_specs=[_const_spec((n, d)), _const_spec(cos.shape), _const_spec(sin.shape),
                  pl.BlockSpec((None, 1, nh, dk, dv), lambda i: (ret_layer, i, 0, 0, 0)),
                  _layer_spec(g.shape, mix_layer), _const_spec(win.shape),
                  _const_spec(wout.shape)] + extra_specs,
        out_specs=[pl.BlockSpec((n, d), lambda i: (0, 0)), out_spec],
        out_shape=[jax.ShapeDtypeStruct((n, d), F32), jax.ShapeDtypeStruct(state.shape, F32)],
        input_output_aliases={7: 1} if extra else {},
        scratch_shapes=[pltpu.VMEM((n, nh * dk), F32), pltpu.VMEM((n, nh * dk), F32),
                        pltpu.VMEM((n, nh * dv), F32), pltpu.VMEM((n, nh * dv), F32),
                        pltpu.VMEM((n, nh * dv), F32)],
        compiler_params=_params(1),
        name="ret_sample",
    )(x.reshape(n, d), cos, sin, state, g, win, wout, *extra)
    return out.reshape(ns, seg, d), stack


def _pool_mixer(x_ref, nm_ref, wp_ref, ps_ref, xp_ref, bo_ref, pos_start):
    nseg, lt, d = x_ref.shape
    gw = d // len(POOL_WINDOWS)
    x = x_ref[...]
    h = _rms(x, nm_ref[...])
    xp_ref[:, POOL_HIST:, :] = h
    pos = pos_start + lax.broadcasted_iota(jnp.int32, (1, lt, 1), 1)
    ys = []
    for gi, w in enumerate(POOL_WINDOWS):
        cols = slice(gi * gw, (gi + 1) * gw)
        acc = xp_ref[:, :, cols]
        span = 1
        while span < w:
            acc = acc + pltpu.roll(acc, span, axis=1)
            span *= 2
        cnt = jnp.minimum(pos + 1, w).astype(F32)
        dg = (acc[:, POOL_HIST:, :] / cnt - h[:, :, cols]).astype(BF16).reshape(nseg * lt, gw)
        ys.append(_dot(dg, wp_ref[gi]))
    y = jnp.concatenate(ys, axis=-1) * ps_ref[...]
    tail = xp_ref[:, lt:lt + POOL_HIST, :]
    bo_ref[...] = tail
    xp_ref[:, 0:POOL_HIST, :] = tail
    return x.reshape(nseg * lt, d) + y


def _pool_ffn_kernel(xp_ref, bufp_ref, xs_ref, bufs_ref, nm_ref, wp_ref, ps_ref, nf_ref, win_ref,
                     wout_ref, nfin_ref, op_ref, bop_ref, os_ref, bos_ref, hist_p_ref, hist_s_ref,
                     act_ref, *, seq_tiles, sample_pos0, final):
    s = pl.program_id(0)
    last = pl.num_programs(0) - 1

    def tile(x_ref, hist_ref, bo_ref, o_ref, act, pos_start):
        nseg, lt, d = x_ref.shape
        x1 = _pool_mixer(x_ref, nm_ref, wp_ref, ps_ref, hist_ref, bo_ref, pos_start)
        x2 = _ffn(x1, nf_ref[...], win_ref, wout_ref, act)
        if final:
            x2 = _rms(x2, nfin_ref[...])
        o_ref[...] = x2.reshape(nseg, lt, d)

    @pl.when(s < last)
    def _():
        j = s % seq_tiles

        @pl.when(j == 0)
        def _():
            hist_p_ref[:, 0:POOL_HIST, :] = bufp_ref[...]

        o2d = op_ref.at[0]
        o2d[...] = _pool_mixer(xp_ref, nm_ref, wp_ref, ps_ref, hist_p_ref, bop_ref,
                               j * xp_ref.shape[1])
        _ffn_refs(o2d, o2d, nf_ref[...], win_ref, wout_ref, act_ref)
        if final:
            o2d[...] = _rms(o2d[...], nfin_ref[...])

    @pl.when(s == last)
    def _():
        hist_s_ref[:, 0:POOL_HIST, :] = bufs_ref[...]
        rows = xs_ref.shape[0] * xs_ref.shape[1]
        tile(xs_ref, hist_s_ref, bos_ref, os_ref, act_ref.at[pl.ds(0, rows)], sample_pos0)


def _pool_ffn_call(xp, bufp, xs, bufs, nm, wp, ps, nf, win, wout, nfin, *, mix_layer, pool_layer,
                   tile, sample_pos0, final, cast_weights=()):
    bp, lp, d = xp.shape
    bs, ls, _ = xs.shape
    seq_tiles = lp // tile
    n_tiles = bp * seq_tiles
    assert bs * ls <= tile
    grid = (n_tiles + 1,)
    casts = _Casts(cast_weights, grid)

    def prompt_tile(s):
        t = jnp.minimum(s, n_tiles - 1)
        return t // seq_tiles, t % seq_tiles

    x_spec = pl.BlockSpec((1, tile, d), lambda s: (*prompt_tile(s), 0))
    hist_spec = pl.BlockSpec((1, POOL_HIST, d), lambda s: (prompt_tile(s)[0], 0, 0))
    return pl.pallas_call(
        casts.wrap(functools.partial(_pool_ffn_kernel, seq_tiles=seq_tiles,
                                     sample_pos0=sample_pos0, final=final), 11, 4),
        grid=grid,
        in_specs=[x_spec, hist_spec, _const_spec(xs.shape), _const_spec(bufs.shape),
                  _layer_spec(nm.shape, mix_layer), _layer_spec(wp.shape, pool_layer),
                  _layer_spec(ps.shape, pool_layer), _layer_spec(nf.shape, mix_layer),
                  _const_spec(win.shape), _const_spec(wout.shape),
                  _const_spec(nfin.shape)] + casts.in_specs,
        out_specs=[x_spec, hist_spec, pl.BlockSpec(xs.shape, lambda s: (0, 0, 0)),
                   pl.BlockSpec(bufs.shape, lambda s: (0, 0, 0))] + casts.out_specs,
        out_shape=[jax.ShapeDtypeStruct(xp.shape, F32),
                   jax.ShapeDtypeStruct((bp, POOL_HIST, d), F32),
                   jax.ShapeDtypeStruct(xs.shape, F32),
                   jax.ShapeDtypeStruct(bufs.shape, F32)] + casts.out_shape,
        scratch_shapes=[pltpu.VMEM((1, POOL_HIST + tile, d), F32),
                        pltpu.VMEM((bs, POOL_HIST + ls, d), F32),
                        pltpu.VMEM((tile, wout.shape[0]), BF16)],
        compiler_params=_params(1),
        name="pool_ffn",
    )(xp, bufp, xs, bufs, nm, wp, ps, nf, win, wout, nfin, *casts.operands)


def _rope_angles(pos, half):
    inv = ROPE_BASE ** (-np.arange(half, dtype=np.float64) / half)
    return np.asarray(pos, np.float64)[:, None] * inv[None, :]


def _cos_sin(ang):
    return jnp.asarray(np.cos(ang), F32), jnp.asarray(np.sin(ang), F32)


def kernel(x_prompt, x_sample, state_ret, state_pool, norm_mix, w_ret_in, w_ret_out, w_pool,
           pool_scale, norm_ffn, w_ffn_in, w_ffn_out, norm_final):
    bp, lp, d = x_prompt.shape
    bs, ls, _ = x_sample.shape
    depth = norm_mix.shape[0]
    n_ret, _, nh, dk, dv = state_ret.shape
    assert nh == RET_HEADS and state_pool.shape[2] == POOL_HIST - 1
    assert lp % (PROMPT_TILE * RET_TILES_PER_STEP) == 0 and PROMPT_TILE % RET_CHUNK == 0
    assert lp % POOL_TILE == 0
    assert ls >= POOL_HIST
    assert (bp * lp) % FFN_TILE == 0
    assert depth % 2 == 0

    ret_w = (w_ret_in[0].astype(BF16), w_ret_out[0].astype(BF16))
    w_pool_b = w_pool.astype(BF16)
    norm_mix3 = norm_mix[:, None, :]
    norm_ffn3 = norm_ffn[:, None, :]
    pool_scale3 = pool_scale[:, None, :]
    nfin = norm_final[None, :]

    half = dk // 2
    cosr, sinr = _cos_sin(_rope_angles(np.arange(PROMPT_TILE), half))
    cosb, sinb = _cos_sin(_rope_angles(np.arange(0, lp, PROMPT_TILE), half)[:, None, :])
    rope_p = (cosr, sinr, cosb, sinb)
    cos_s, sin_s = _cos_sin(np.tile(_rope_angles(PAST_LEN + np.arange(ls), half), (bs, 1)))

    pool0_p = jnp.zeros((bp, POOL_HIST, d), state_pool.dtype)
    pool0_s = jnp.pad(state_pool, ((0, 0), (0, 0), (1, 0), (0, 0)))

    xp, xs = x_prompt, x_sample
    ret_p, pool_p, pool_s = [], [], []
    ret_s = None

    def ffn_weights(layer):
        return [(w_ffn_in, layer), (w_ffn_out, layer)]

    for i in range(depth):
        jm = i // 2
        if i % 2 == 0:
            xp, sp, *ffn_w = _ret_prompt_call(xp, rope_p, (nh, dk, dv), norm_mix3, *ret_w, i,
                                              cast_weights=ffn_weights(i))
            ret_p.append(sp)
            xs, ret_s = _ret_sample_call(xs, cos_s, sin_s, state_ret, norm_mix3, *ret_w, i, jm,
                                         ret_s)
            xp, xs, *ffn_w = _ffn_call(xp.reshape(bp * lp, d), xs.reshape(bs * ls, d), norm_ffn3,
                                       *ffn_w, i, FFN_TILE, cast_weights=ffn_weights(i + 1))
            xp, xs = xp.reshape(bp, lp, d), xs.reshape(bs, ls, d)
        else:
            next_ret = [(w_ret_in, jm + 1), (w_ret_out, jm + 1)] if i + 1 < depth else []
            xp, bpn, xs, bsn, *ret_w = _pool_ffn_call(
                xp, pool0_p, xs, pool0_s[jm], norm_mix3, w_pool_b, pool_scale3, norm_ffn3, *ffn_w,
                nfin, mix_layer=i, pool_layer=jm, tile=POOL_TILE, sample_pos0=PAST_LEN,
                final=i == depth - 1, cast_weights=next_ret)
            pool_p.append(bpn[:, 1:, :])
            pool_s.append(bsn[:, 1:, :])
    return (xp, xs, jnp.stack(ret_p), ret_s, jnp.stack(pool_p), jnp.stack(pool_s))
```
